```python
import math
import jax, jax.numpy as jnp
from jax import lax
import numpy as np

D_MODEL = 1024
BATCH = 1
SEQ = 16384
DEPTH = 1
DEC_BATCH = 8
DEC_SEQ = 64
PAST_LEN = 1024

CHUNK = 64
Q_BLOCK = 128
A_HEADS = 4
A_QK_DIM = 64
A_V_DIM = 2 * A_QK_DIM
B_HEADS = 8
B_DIM = 64
A_WIDTH = A_HEADS * A_V_DIM
B_WIDTH = B_HEADS * B_DIM
MIX_WIDTH = A_WIDTH + B_WIDTH
ROT_DIM = A_QK_DIM // 4
ROPE_THETA = 500000.0
EPS = 1e-6
A_QK_WIDTH = A_HEADS * 2 * A_QK_DIM
IN_WIDTH = 2 * A_QK_WIDTH + 2 * A_WIDTH + 4 * B_WIDTH + B_HEADS
FORGET_BIAS = 3.0

kernel_name = "hybrid_diff_fox_stream_step"

F32 = jnp.float32


def rms_norm(x, g):
    xf = x.astype(F32)
    y = xf * lax.rsqrt(jnp.mean(xf * xf, axis=-1, keepdims=True) + EPS)
    return y.astype(x.dtype) * g


def rope(x, pos):
    t = x.shape[1]
    half = ROT_DIM // 2
    inv = ROPE_THETA ** (-jnp.arange(0, ROT_DIM, 2, dtype=F32) / ROT_DIM)
    ang = pos.astype(F32)[:, None] * inv[None, :]
    shape = (1, t) + (1,) * (x.ndim - 3) + (half,)
    cos = jnp.cos(ang).reshape(shape).astype(x.dtype)
    sin = jnp.sin(ang).reshape(shape).astype(x.dtype)
    x1 = x[..., :half]
    x2 = x[..., half:ROT_DIM]
    return jnp.concatenate([x1 * cos - x2 * sin, x2 * cos + x1 * sin, x[..., ROT_DIM:]], axis=-1)


def split_offsets():
    sizes = [A_QK_WIDTH, A_QK_WIDTH, A_WIDTH, A_WIDTH, B_WIDTH, B_WIDTH, B_WIDTH, B_HEADS, B_WIDTH]
    offs = []
    s = 0
    for n in sizes[:-1]:
        s += n
        offs.append(s)
    return offs


def project(x, c, pos, norm_g, w_ada, b_ada, w_in, b_f, qn_a, kn_a, qn_b, kn_b):
    b, t = x.shape[:2]
    mod = jax.nn.silu(c) @ w_ada + b_ada
    shift, scale, gate = jnp.split(mod, 3, axis=-1)
    h = rms_norm(x, norm_g) * (1.0 + scale[:, None, :]) + shift[:, None, :]
    u = h @ w_in
    qa, ka, va, za, qb, kb, vb, fb, zb = jnp.split(u, split_offsets(), axis=-1)
    qa = rope(rms_norm(qa.reshape(b, t, A_HEADS, 2, A_QK_DIM), qn_a), pos)
    ka = rope(rms_norm(ka.reshape(b, t, A_HEADS, 2, A_QK_DIM), kn_a), pos)
    va = va.reshape(b, t, A_HEADS, A_V_DIM)
    qb = rms_norm(qb.reshape(b, t, B_HEADS, B_DIM), qn_b)
    kb = rms_norm(kb.reshape(b, t, B_HEADS, B_DIM), kn_b)
    vb = vb.reshape(b, t, B_HEADS, B_DIM)
    logf = jax.nn.log_sigmoid((fb + b_f).astype(F32))
    return gate, qa, ka, va, za, qb, kb, vb, zb, logf


def diff_attention(q, k, v, mask, lam):
    s = jnp.einsum('bqhjd,bkhjd->bhjqk', q, k).astype(F32) * (A_QK_DIM ** -0.5)
    p = jax.nn.softmax(jnp.where(mask, s, -jnp.inf), axis=-1)
    a = p[:, :, 0] - lam * p[:, :, 1]
    return jnp.einsum('bhqk,bkhe->bqhe', a.astype(v.dtype), v)


def forgetting_attention(q, k, v, mask, bias):
    s = jnp.einsum('bqhd,bkhd->bhqk', q, k).astype(F32) * (B_DIM ** -0.5) + bias
    p = jax.nn.softmax(jnp.where(mask, s, -jnp.inf), axis=-1)
    return jnp.einsum('bhqk,bkhd->bqhd', p.astype(v.dtype), v)


def merge(x, gate, oa, za, ob, zb, subln_g, lam_init, w_out):
    b, t = x.shape[:2]
    oa = rms_norm(oa, subln_g) * (1.0 - lam_init)
    ga = oa.reshape(b, t, A_WIDTH) * jax.nn.silu(za)
    gb = ob.reshape(b, t, B_WIDTH) * jax.nn.silu(zb)
    o = jnp.concatenate([ga, gb], axis=-1) @ w_out
    return x + gate[:, None, :] * o


def setup_inputs(seed: int = 0) -> dict:
    key = jax.random.key(seed)
    ks = jax.random.split(key, 32)
    nrm = jax.random.normal
    d = D_MODEL
    inp = {}
    inp['x_prompt'] = nrm(ks[0], (BATCH, SEQ, d), F32)
    inp['x_sample'] = nrm(ks[1], (DEC_BATCH, DEC_SEQ, d), F32)
    inp['cache_a_k'] = nrm(ks[2], (DEPTH, DEC_BATCH, PAST_LEN, A_HEADS, 2, A_QK_DIM), F32)
    inp['cache_a_v'] = nrm(ks[3], (DEPTH, DEC_BATCH, PAST_LEN, A_HEADS, A_V_DIM), F32)
    inp['cache_b_k'] = nrm(ks[4], (DEPTH, DEC_BATCH, PAST_LEN, B_HEADS, B_DIM), F32)
    inp['cache_b_v'] = nrm(ks[5], (DEPTH, DEC_BATCH, PAST_LEN, B_HEADS, B_DIM), F32)
    inp['cache_b_logf'] = jax.nn.log_sigmoid(FORGET_BIAS + nrm(ks[6], (DEPTH, DEC_BATCH, PAST_LEN, B_HEADS), F32))
    inp['c_prompt'] = nrm(ks[7], (BATCH, d), F32)
    inp['c_sample'] = nrm(ks[8], (DEC_BATCH, d), F32)
    inp['norm_g'] = 1.0 + 0.02 * nrm(ks[9], (DEPTH, d), F32)
    inp['w_ada'] = nrm(ks[10], (DEPTH, d, 3 * d), F32) * d ** -0.5
    inp['b_ada'] = 0.01 * nrm(ks[11], (DEPTH, 3 * d), F32)
    inp['w_in'] = nrm(ks[12], (DEPTH, d, IN_WIDTH), F32) * d ** -0.5
    inp['b_f'] = FORGET_BIAS + 0.1 * nrm(ks[13], (DEPTH, B_HEADS), F32)
    inp['qn_a'] = 1.0 + 0.02 * nrm(ks[14], (DEPTH, A_QK_DIM), F32)
    inp['kn_a'] = 1.0 + 0.02 * nrm(ks[15], (DEPTH, A_QK_DIM), F32)
    inp['lam_q1'] = 0.1 * nrm(ks[16], (DEPTH, A_QK_DIM), F32)
    inp['lam_k1'] = 0.1 * nrm(ks[17], (DEPTH, A_QK_DIM), F32)
    inp['lam_q2'] = 0.1 * nrm(ks[18], (DEPTH, A_QK_DIM), F32)
    inp['lam_k2'] = 0.1 * nrm(ks[19], (DEPTH, A_QK_DIM), F32)
    inp['subln_g'] = 1.0 + 0.02 * nrm(ks[20], (DEPTH, A_V_DIM), F32)
    inp['qn_b'] = 1.0 + 0.02 * nrm(ks[21], (DEPTH, B_DIM), F32)
    inp['kn_b'] = 1.0 + 0.02 * nrm(ks[22], (DEPTH, B_DIM), F32)
    inp['w_out'] = nrm(ks[23], (DEPTH, MIX_WIDTH, d), F32) * MIX_WIDTH ** -0.5
    return inp


def reference(x_prompt, x_sample, cache_a_k, cache_a_v, cache_b_k, cache_b_v, cache_b_logf,
              c_prompt, c_sample, norm_g, w_ada, b_ada, w_in, b_f, qn_a, kn_a,
              lam_q1, lam_k1, lam_q2, lam_k2, subln_g, qn_b, kn_b, w_out):
    xp = x_prompt
    xs = x_sample
    bp, seq = xp.shape[:2]
    bs, tdec = xs.shape[:2]
    past = cache_a_k.shape[2]
    n_blocks = seq // Q_BLOCK
    pos_p = jnp.arange(seq)
    pos_s = past + jnp.arange(tdec)
    kpos_s = jnp.arange(past + tdec)
    mask_a_s = (kpos_s // CHUNK)[None, :] <= (pos_s // CHUNK)[:, None]
    mask_b_s = kpos_s[None, :] <= pos_s[:, None]
    pa_k, pa_v, pb_k, pb_v, pb_f = [], [], [], [], []
    sa_k, sa_v, sb_k, sb_v, sb_f = [], [], [], [], []
    for l in range(DEPTH):
        lam_init = 0.8 - 0.6 * math.exp(-0.3 * l)
        lam = (jnp.exp(jnp.sum((lam_q1[l] * lam_k1[l]).astype(F32)))
               - jnp.exp(jnp.sum((lam_q2[l] * lam_k2[l]).astype(F32))) + lam_init)
        lw = (norm_g[l], w_ada[l], b_ada[l], w_in[l], b_f[l], qn_a[l], kn_a[l], qn_b[l], kn_b[l])

        gate, qa, ka, va, za, qb, kb, vb, zb, logf = project(xp, c_prompt, pos_p, *lw)
        cum_t = jnp.cumsum(logf, axis=1).transpose(0, 2, 1)

        def prompt_block(i):
            start = i * Q_BLOCK
            qpos = start + jnp.arange(Q_BLOCK)
            mask_a = (pos_p // CHUNK)[None, :] <= (qpos // CHUNK)[:, None]
            oa_blk = diff_attention(lax.dynamic_slice_in_dim(qa, start, Q_BLOCK, axis=1), ka, va, mask_a, lam)
            cq = lax.dynamic_slice_in_dim(cum_t, start, Q_BLOCK, axis=2)
            bias = cq[:, :, :, None] - cum_t[:, :, None, :]
            mask_b = pos_p[None, :] <= qpos[:, None]
            ob_blk = forgetting_attention(lax.dynamic_slice_in_dim(qb, start, Q_BLOCK, axis=1), kb, vb, mask_b, bias)
            return oa_blk, ob_blk

        oa_b, ob_b = lax.map(prompt_block, jnp.arange(n_blocks))
        oa = jnp.moveaxis(oa_b, 0, 1).reshape(bp, seq, A_HEADS, A_V_DIM)
        ob = jnp.moveaxis(ob_b, 0, 1).reshape(bp, seq, B_HEADS, B_DIM)
        xp = merge(xp, gate, oa, za, ob, zb, subln_g[l], lam_init, w_out[l])
        pa_k.append(ka); pa_v.append(va); pb_k.append(kb); pb_v.append(vb); pb_f.append(logf)

        gate_s, qa_s, ka_s, va_s, za_s, qb_s, kb_s, vb_s, zb_s, logf_s = project(xs, c_sample, pos_s, *lw)
        ka_all = jnp.concatenate([cache_a_k[l].astype(ka_s.dtype), ka_s], axis=1)
        va_all = jnp.concatenate([cache_a_v[l].astype(va_s.dtype), va_s], axis=1)
        oa_s = diff_attention(qa_s, ka_all, va_all, mask_a_s, lam)
        kb_all = jnp.concatenate([cache_b_k[l].astype(kb_s.dtype), kb_s], axis=1)
        vb_all = jnp.concatenate([cache_b_v[l].astype(vb_s.dtype), vb_s], axis=1)
        cum_s = jnp.cumsum(jnp.concatenate([cache_b_logf[l].astype(F32), logf_s], axis=1), axis=1).transpose(0, 2, 1)
        bias_s = cum_s[:, :, past:, None] - cum_s[:, :, None, :]
        ob_s = forgetting_attention(qb_s, kb_all, vb_all, mask_b_s, bias_s)
        xs = merge(xs, gate_s, oa_s, za_s, ob_s, zb_s, subln_g[l], lam_init, w_out[l])
        sa_k.append(ka_s); sa_v.append(va_s); sb_k.append(kb_s); sb_v.append(vb_s); sb_f.append(logf_s)

    return (xp, xs,
            jnp.stack(pa_k), jnp.stack(pa_v), jnp.stack(pb_k), jnp.stack(pb_v), jnp.stack(pb_f),
            jnp.stack(sa_k), jnp.stack(sa_v), jnp.stack(sb_k), jnp.stack(sb_v), jnp.stack(sb_f))
```

```python
import functools
import math

import numpy as np
import jax
import jax.numpy as jnp
from jax import lax
from jax.experimental import pallas as pl
from jax.experimental.pallas import tpu as pltpu

F32 = jnp.float32
BF16 = jnp.bfloat16

CHUNK = 64
A_HEADS = 4
A_QK_DIM = 64
A_V_DIM = 2 * A_QK_DIM
B_HEADS = 8
B_DIM = 64
A_WIDTH = A_HEADS * A_V_DIM
B_WIDTH = B_HEADS * B_DIM
A_QK_WIDTH = A_HEADS * 2 * A_QK_DIM
ROT_DIM = A_QK_DIM // 4
ROPE_THETA = 500000.0
EPS = 1e-6

LANES = 128
MXU_DIM = 256
VMEM_LIMIT = 56 * 1024 * 1024

PROJ_ROWS = 256
ATTN_TILE = 512
OUT_ROWS = 512

_NT = (((1,), (1,)), ((), ()))


def _params(n_axes):
    return pltpu.CompilerParams(dimension_semantics=("arbitrary",) * n_axes,
                                vmem_limit_bytes=VMEM_LIMIT)


def _silu(x):
    return x * jax.nn.sigmoid(x)


def _mod_kernel(c_ref, w_ref, b_ref, o_ref):
    a = _silu(c_ref[...]).astype(BF16)
    o_ref[...] = jnp.dot(a, w_ref[...].astype(BF16), preferred_element_type=F32) + b_ref[...]


def _modulation(c_all, w_ada, b_ada):
    rows, d = c_all.shape
    n = w_ada.shape[1]
    bn = 512
    return pl.pallas_call(
        _mod_kernel,
        grid=(n // bn,),
        in_specs=[pl.BlockSpec((rows, d), lambda j: (0, 0)),
                  pl.BlockSpec((d, bn), lambda j: (0, j)),
                  pl.BlockSpec((1, bn), lambda j: (0, j))],
        out_specs=pl.BlockSpec((rows, bn), lambda j: (0, j)),
        out_shape=jax.ShapeDtypeStruct((rows, n), F32),
        compiler_params=_params(1),
        name="modulation",
    )(c_all, w_ada, b_ada.reshape(1, n))


def _group_rms(u, gm):
    sq = u * u
    hi = sq.astype(BF16)
    lo = (sq - hi.astype(F32)).astype(BF16)
    parts = []
    for c in range(u.shape[1] // MXU_DIM):
        sl = slice(MXU_DIM * c, MXU_DIM * (c + 1))
        parts.append(jnp.dot(hi[:, sl], gm, preferred_element_type=F32)
                     + jnp.dot(lo[:, sl], gm, preferred_element_type=F32))
    ms = jnp.concatenate(parts, axis=1)
    return u * lax.rsqrt(ms + EPS)


def _rope(n, c, s1, s2):
    reps = n.shape[1] // LANES
    c = jnp.concatenate([c] * reps, axis=1)
    s1 = jnp.concatenate([s1] * reps, axis=1)
    s2 = jnp.concatenate([s2] * reps, axis=1)
    half = ROT_DIM // 2
    return (n * c + pltpu.roll(n, n.shape[1] - half, 1) * s1 + pltpu.roll(n, half, 1) * s2)


def _proj_kernel(x_ref, scale_ref, shift_ref, g_ref, w_ref, wf_ref, bf_ref,
                 qna_ref, kna_ref, qnb_ref, knb_ref, rc_ref, rs1_ref, rs2_ref, gm_ref,
                 qa_o, ka_o, kab_o, va_o, vab_o, za_o, qb_o, kb_o, kbb_o, vb_o, vbaug_o, zb_o,
                 lf_o, cum_o, carry_ref):
    tm = x_ref.shape[0]
    x = x_ref[...]
    y = x * lax.rsqrt(jnp.mean(x * x, axis=-1, keepdims=True) + EPS) * g_ref[...]
    h = y * (1.0 + scale_ref[...]) + shift_ref[...]
    hb = h.astype(BF16)
    gm = gm_ref[...]
    rc, rs1, rs2 = rc_ref[...], rs1_ref[...], rs2_ref[...]
    qk_scale = A_QK_DIM ** -0.5

    def group(i):
        return jnp.dot(hb, w_ref[:, 512 * i:512 * (i + 1)], preferred_element_type=F32)

    qa = _rope(_group_rms(group(0), gm) * qna_ref[...], rc, rs1, rs2)
    qa_o[...] = (qa * qk_scale).astype(BF16)
    ka = _rope(_group_rms(group(1), gm) * kna_ref[...], rc, rs1, rs2)
    ka_o[...] = ka
    kab_o[...] = ka.astype(BF16)
    va = group(2)
    va_o[...] = va
    vab_o[...] = va.astype(BF16)
    za_o[...] = group(3)
    qb = _group_rms(group(4), gm) * qnb_ref[...]
    qb_o[...] = (qb * qk_scale).astype(BF16)
    kb = _group_rms(group(5), gm) * knb_ref[...]
    kb_o[...] = kb
    kbb_o[...] = kb.astype(BF16)
    vb = group(6)
    vb_o[...] = vb
    lane = lax.broadcasted_iota(jnp.int32, (tm, LANES), 1)
    for p in range(B_HEADS // 2):
        blk = vb[:, LANES * p:LANES * (p + 1)]
        even = jnp.where(lane < B_DIM, blk, jnp.where(lane == B_DIM, 1.0, 0.0))
        odd = jnp.where(lane >= B_DIM, blk, jnp.where(lane == 0, 1.0, 0.0))
        vbaug_o[:, 2 * LANES * p:2 * LANES * p + LANES] = even.astype(BF16)
        vbaug_o[:, 2 * LANES * p + LANES:2 * LANES * (p + 1)] = odd.astype(BF16)
    zb_o[...] = group(7)

    fb = lax.dot_general(wf_ref[...], hb, _NT, preferred_element_type=F32)
    fx = fb + bf_ref[...]
    logf = jnp.minimum(fx, 0.0) - jnp.log1p(jnp.exp(-jnp.abs(fx)))
    lf_o[...] = logf[:B_HEADS]

    @pl.when(pl.program_id(0) == 0)
    def _():
        carry_ref[...] = jnp.zeros_like(carry_ref)

    pos = lax.broadcasted_iota(jnp.int32, logf.shape, 1)
    c = logf
    s = 1
    while s < tm:
        c = c + jnp.where(pos >= s, pltpu.roll(c, s, 1), 0.0)
        s *= 2
    c = c + carry_ref[...]
    cum_o[...] = c[:B_HEADS]
    carry_ref[...] = c[:, tm - 1:tm]


def _projection(x2, scale, shift, norm_g, w_main, wf_t, bf_col, gains, rope_tabs, gm):
    t, d = x2.shape
    tm = PROJ_ROWS
    per_row = scale.shape[0] != 1
    mod_spec = (pl.BlockSpec((tm, d), lambda i: (i, 0)) if per_row
                else pl.BlockSpec((1, d), lambda i: (0, 0)))
    const = lambda shape: pl.BlockSpec(shape, lambda i: (0, 0))
    row = lambda width: pl.BlockSpec((tm, width), lambda i: (i, 0))
    lane_blk = pl.BlockSpec((B_HEADS, tm), lambda i: (0, i))
    in_specs = [row(d), mod_spec, mod_spec, const((1, d)),
                const(w_main.shape), const(wf_t.shape), const(bf_col.shape),
                const((1, 512)), const((1, 512)), const((1, 512)), const((1, 512)),
                row(LANES), row(LANES), row(LANES), const(gm.shape)]
    f32o = lambda w: jax.ShapeDtypeStruct((t, w), F32)
    bf16o = lambda w: jax.ShapeDtypeStruct((t, w), BF16)
    out_shape = [bf16o(512), f32o(512), bf16o(512), f32o(512), bf16o(512), f32o(512),
                 bf16o(512), f32o(512), bf16o(512), f32o(512), bf16o(1024), f32o(512),
                 jax.ShapeDtypeStruct((B_HEADS, t), F32), jax.ShapeDtypeStruct((B_HEADS, t), F32)]
    out_specs = [row(512)] * 10 + [row(1024), row(512), lane_blk, lane_blk]
    return pl.pallas_call(
        _proj_kernel,
        grid=(t // tm,),
        in_specs=in_specs,
        out_specs=out_specs,
        out_shape=out_shape,
        scratch_shapes=[pltpu.VMEM((16, 1), F32)],
        compiler_params=_params(1),
        name="projection",
    )(x2, scale, shift, norm_g, w_main, wf_t, bf_col, *gains, *rope_tabs, gm)


def _online_tile(s, v, t, m_sc, l_sc, acc_sc):
    m_prev = m_sc[t]
    m_new = jnp.maximum(m_prev, jnp.max(s, axis=1, keepdims=True))
    alpha = jnp.exp(m_prev - m_new)
    p = jnp.exp(s - m_new)
    if l_sc is not None:
        l_sc[t] = alpha * l_sc[t] + jnp.sum(p, axis=1, keepdims=True)
    acc_sc[t] = alpha * acc_sc[t] + jnp.dot(p.astype(BF16), v, preferred_element_type=F32)
    m_sc[t] = m_new


def _lam(lq1_ref, lk1_ref, lq2_ref, lk2_ref, lam_init):
    return (jnp.exp(jnp.sum(lq1_ref[...] * lk1_ref[...], axis=1, keepdims=True))
            - jnp.exp(jnp.sum(lq2_ref[...] * lk2_ref[...], axis=1, keepdims=True)) + lam_init)


def _diff_epilogue(o0, o1, lam, subln_g, lam_init, z):
    oa = o0 - lam * o1
    on = oa * lax.rsqrt(jnp.mean(oa * oa, axis=-1, keepdims=True) + EPS)
    return on * subln_g * (1.0 - lam_init) * _silu(z)


def _attn_a_kernel(q_ref, k_ref, v_ref, z_ref, lq1_ref, lk1_ref, lq2_ref, lk2_ref, sg_ref,
                   o_ref, m_sc, l_sc, acc_sc, *, lam_init):
    tq = q_ref.shape[0]
    qi = pl.program_id(1)
    q = q_ref[...]
    lane = lax.broadcasted_iota(jnp.int32, q.shape, 1)
    zero = jnp.zeros_like(q)
    qs = (jnp.where(lane < A_QK_DIM, q, zero), jnp.where(lane >= A_QK_DIM, q, zero))
    m_sc[...] = jnp.full(m_sc.shape, -jnp.inf, F32)
    l_sc[...] = jnp.zeros_like(l_sc)
    acc_sc[...] = jnp.zeros_like(acc_sc)

    def tile(j, diagonal):
        off = pl.multiple_of(j * tq, tq)
        k = k_ref[pl.ds(off, tq), :]
        v = v_ref[pl.ds(off, tq), :]
        for t in range(2):
            s = lax.dot_general(qs[t], k, _NT, preferred_element_type=F32)
            if diagonal:
                r = lax.broadcasted_iota(jnp.int32, s.shape, 0) // CHUNK
                c = lax.broadcasted_iota(jnp.int32, s.shape, 1) // CHUNK
                s = jnp.where(c <= r, s, -jnp.inf)
            _online_tile(s, v, t, m_sc, l_sc, acc_sc)

    def body(j, carry):
        tile(j, False)
        return carry

    lax.fori_loop(0, qi, body, 0)
    tile(qi, True)

    lam = _lam(lq1_ref, lk1_ref, lq2_ref, lk2_ref, lam_init)
    out = _diff_epilogue(acc_sc[0] / l_sc[0], acc_sc[1] / l_sc[1], lam, sg_ref[...], lam_init,
                         z_ref[...])
    o_ref[...] = out.astype(o_ref.dtype)


def _attn_a_prompt(qa, kab, vab, za, lam_vecs, subln_g, lam_init):
    t = qa.shape[0]
    tq = ATTN_TILE
    vec = pl.BlockSpec((1, A_QK_DIM), lambda h, i: (0, 0))
    return pl.pallas_call(
        functools.partial(_attn_a_kernel, lam_init=lam_init),
        grid=(A_HEADS, t // tq),
        in_specs=[pl.BlockSpec((tq, LANES), lambda h, i: (i, h)),
                  pl.BlockSpec((t, LANES), lambda h, i: (0, h)),
                  pl.BlockSpec((t, LANES), lambda h, i: (0, h)),
                  pl.BlockSpec((tq, LANES), lambda h, i: (i, h)),
                  vec, vec, vec, vec,
                  pl.BlockSpec((1, A_V_DIM), lambda h, i: (0, 0))],
        out_specs=pl.BlockSpec((tq, LANES), lambda h, i: (i, h)),
        out_shape=jax.ShapeDtypeStruct((t, A_WIDTH), BF16),
        scratch_shapes=[pltpu.VMEM((2, tq, 1), F32), pltpu.VMEM((2, tq, 1), F32),
                        pltpu.VMEM((2, tq, LANES), F32)],
        compiler_params=_params(2),
        name="attn_a_prompt",
    )(qa, kab, vab, za, *lam_vecs, subln_g)


def _fox_epilogue(acc0, acc1, z):
    lane = lax.broadcasted_iota(jnp.int32, acc0.shape, 1)
    o = jnp.where(lane < B_DIM, acc0 / acc0[:, B_DIM:B_DIM + 1], acc1 / acc1[:, 0:1])
    return o * _silu(z)


def _attn_b_kernel(q_ref, k_ref, v0_ref, v1_ref, cum_ref, z_ref, o_ref, m_sc, acc_sc):
    tq = q_ref.shape[0]
    pair = pl.program_id(0)
    qi = pl.program_id(1)
    q = q_ref[...]
    lane = lax.broadcasted_iota(jnp.int32, q.shape, 1)
    zero = jnp.zeros_like(q)
    qs = (jnp.where(lane < B_DIM, q, zero), jnp.where(lane >= B_DIM, q, zero))
    vs = (v0_ref, v1_ref)
    m_sc[...] = jnp.full(m_sc.shape, -jnp.inf, F32)
    acc_sc[...] = jnp.zeros_like(acc_sc)

    def tile(j, diagonal):
        off = pl.multiple_of(j * tq, tq)
        k = k_ref[pl.ds(off, tq), :]
        for t in range(2):
            s = lax.dot_general(qs[t], k, _NT, preferred_element_type=F32)
            s = s - cum_ref[pl.ds(2 * pair + t, 1), pl.ds(off, tq)]
            if diagonal:
                r = lax.broadcasted_iota(jnp.int32, s.shape, 0)
                c = lax.broadcasted_iota(jnp.int32, s.shape, 1)
                s = jnp.where(c <= r, s, -jnp.inf)
            _online_tile(s, vs[t][pl.ds(off, tq), :], t, m_sc, None, acc_sc)

    def body(j, carry):
        tile(j, False)
        return carry

    lax.fori_loop(0, qi, body, 0)
    tile(qi, True)
    o_ref[...] = _fox_epilogue(acc_sc[0], acc_sc[1], z_ref[...]).astype(o_ref.dtype)


def _attn_b_prompt(qb, kbb, vbaug, cum_t, zb):
    t = qb.shape[0]
    tq = ATTN_TILE
    return pl.pallas_call(
        _attn_b_kernel,
        grid=(B_HEADS // 2, t // tq),
        in_specs=[pl.BlockSpec((tq, LANES), lambda p, i: (i, p)),
                  pl.BlockSpec((t, LANES), lambda p, i: (0, p)),
                  pl.BlockSpec((t, LANES), lambda p, i: (0, 2 * p)),
                  pl.BlockSpec((t, LANES), lambda p, i: (0, 2 * p + 1)),
                  pl.BlockSpec((B_HEADS, t), lambda p, i: (0, 0)),
                  pl.BlockSpec((tq, LANES), lambda p, i: (i, p))],
        out_specs=pl.BlockSpec((tq, LANES), lambda p, i: (i, p)),
        out_shape=jax.ShapeDtypeStruct((t, B_WIDTH), BF16),
        scratch_shapes=[pltpu.VMEM((2, tq, 1), F32), pltpu.VMEM((2, tq, LANES), F32)],
        compiler_params=_params(2),
        name="attn_b_prompt",
    )(qb, kbb, vbaug, vbaug, cum_t, zb)


def _softmax_two_part(s_c, s_n, v_c, v_n, want_l):
    m = jnp.maximum(jnp.max(s_c, axis=1, keepdims=True), jnp.max(s_n, axis=1, keepdims=True))
    p_c = jnp.exp(s_c - m)
    p_n = jnp.exp(s_n - m)
    acc = (jnp.dot(p_c.astype(BF16), v_c, preferred_element_type=F32)
           + jnp.dot(p_n.astype(BF16), v_n, preferred_element_type=F32))
    l = None
    if want_l:
        l = jnp.sum(p_c, axis=1, keepdims=True) + jnp.sum(p_n, axis=1, keepdims=True)
    return acc, l


def _attn_dec_kernel(qa_ref, kan_ref, van_ref, za_ref, qb_ref, kbn_ref, vbn_ref, zb_ref,
                     cak_ref, cav_ref, cbk_ref, cbv_ref, lf_ref,
                     lq1_ref, lk1_ref, lq2_ref, lk2_ref, sg_ref,
                     ga_o, gb_o, *, lam_init):
    tdec = qa_ref.shape[0]
    past = cak_ref.shape[1]
    lane = lax.broadcasted_iota(jnp.int32, (tdec, LANES), 1)
    lam = _lam(lq1_ref, lk1_ref, lq2_ref, lk2_ref, lam_init)

    for h in range(A_HEADS):
        cols = slice(LANES * h, LANES * (h + 1))
        q = qa_ref[:, cols]
        zero = jnp.zeros_like(q)
        k_c = cak_ref[0, :, cols].astype(BF16)
        v_c = cav_ref[0, :, cols].astype(BF16)
        k_n = kan_ref[:, cols]
        v_n = van_ref[:, cols]
        outs = []
        for t in range(2):
            qm = jnp.where((lane >= A_QK_DIM) == (t == 1), q, zero)
            s_c = lax.dot_general(qm, k_c, _NT, preferred_element_type=F32)
            s_n = lax.dot_general(qm, k_n, _NT, preferred_element_type=F32)
            acc, l = _softmax_two_part(s_c, s_n, v_c, v_n, True)
            outs.append(acc / l)
        ga_o[:, cols] = _diff_epilogue(outs[0], outs[1], lam, sg_ref[...], lam_init,
                                       za_ref[:, cols]).astype(ga_o.dtype)

    lf = lf_ref[0]
    pos = lax.broadcasted_iota(jnp.int32, lf.shape, 1)
    cum = lf
    s = 1
    while s < lf.shape[1]:
        cum = cum + jnp.where(pos >= s, pltpu.roll(cum, s, 1), 0.0)
        s *= 2
    row = lax.broadcasted_iota(jnp.int32, (tdec, tdec), 0)
    col = lax.broadcasted_iota(jnp.int32, (tdec, tdec), 1)
    lane_c = lax.broadcasted_iota(jnp.int32, (past, LANES), 1)
    for p in range(B_HEADS // 2):
        cols = slice(LANES * p, LANES * (p + 1))
        q = qb_ref[:, cols]
        zero = jnp.zeros_like(q)
        k_c = cbk_ref[0, :, cols].astype(BF16)
        k_n = kbn_ref[:, cols]
        v_c = cbv_ref[0, :, cols]
        accs = []
        for t in range(2):
            qm = jnp.where((lane >= B_DIM) == (t == 1), q, zero)
            if t == 0:
                v_aug = jnp.where(lane_c < B_DIM, v_c, jnp.where(lane_c == B_DIM, 1.0, 0.0))
            else:
                v_aug = jnp.where(lane_c >= B_DIM, v_c, jnp.where(lane_c == 0, 1.0, 0.0))
            head = 2 * p + t
            s_c = lax.dot_general(qm, k_c, _NT, preferred_element_type=F32)
            s_c = s_c - cum[head:head + 1, :past]
            s_n = lax.dot_general(qm, k_n, _NT, preferred_element_type=F32)
            s_n = s_n - cum[head:head + 1, past:past + tdec]
            s_n = jnp.where(col <= row, s_n, -jnp.inf)
            v_n = vbn_ref[:, LANES * head:LANES * (head + 1)]
            acc, _ = _softmax_two_part(s_c, s_n, v_aug.astype(BF16), v_n, False)
            accs.append(acc)
        gb_o[:, cols] = _fox_epilogue(accs[0], accs[1], zb_ref[:, cols]).astype(gb_o.dtype)


def _attn_decode(proj, caches, lf_cat, lam_vecs, subln_g, lam_init, nb, tdec):
    qa, kab, vab, za, qb, kbb, vbaug, zb = proj
    cak, cav, cbk, cbv = caches
    past = cak.shape[1]
    rows = lambda w: pl.BlockSpec((tdec, w), lambda b: (b, 0))
    cache = pl.BlockSpec((1, past, 512), lambda b: (b, 0, 0))
    vec = pl.BlockSpec((1, A_QK_DIM), lambda b: (0, 0))
    return pl.pallas_call(
        functools.partial(_attn_dec_kernel, lam_init=lam_init),
        grid=(nb,),
        in_specs=[rows(512), rows(512), rows(512), rows(512),
                  rows(512), rows(512), rows(1024), rows(512),
                  cache, cache, cache, cache,
                  pl.BlockSpec((1, B_HEADS, lf_cat.shape[2]), lambda b: (b, 0, 0)),
                  vec, vec, vec, vec, pl.BlockSpec((1, A_V_DIM), lambda b: (0, 0))],
        out_specs=[rows(512), rows(512)],
        out_shape=[jax.ShapeDtypeStruct((nb * tdec, A_WIDTH), BF16),
                   jax.ShapeDtypeStruct((nb * tdec, B_WIDTH), BF16)],
        compiler_params=_params(1),
        name="attn_decode",
    )(qa, kab, vab, za, qb, kbb, vbaug, zb, cak, cav, cbk, cbv, lf_cat, *lam_vecs, subln_g)


def _out_kernel(x_ref, gate_ref, ga_ref, gb_ref, w_ref, y_ref):
    o = (jnp.dot(ga_ref[...], w_ref[:A_WIDTH, :], preferred_element_type=F32)
         + jnp.dot(gb_ref[...], w_ref[A_WIDTH:, :], preferred_element_type=F32))
    y_ref[...] = x_ref[...] + gate_ref[...] * o


def _out_projection(x2, gate, ga, gb, w_out):
    t, d = x2.shape
    tm = OUT_ROWS
    gate_spec = (pl.BlockSpec((tm, d), lambda i: (i, 0)) if gate.shape[0] != 1
                 else pl.BlockSpec((1, d), lambda i: (0, 0)))
    return pl.pallas_call(
        _out_kernel,
        grid=(t // tm,),
        in_specs=[pl.BlockSpec((tm, d), lambda i: (i, 0)), gate_spec,
                  pl.BlockSpec((tm, A_WIDTH), lambda i: (i, 0)),
                  pl.BlockSpec((tm, B_WIDTH), lambda i: (i, 0)),
                  pl.BlockSpec(w_out.shape, lambda i: (0, 0))],
        out_specs=pl.BlockSpec((tm, d), lambda i: (i, 0)),
        out_shape=jax.ShapeDtypeStruct((t, d), F32),
        compiler_params=_params(1),
        name="out_projection",
    )(x2, gate, ga, gb, w_out)


def _rope_tables(pos):
    half = ROT_DIM // 2
    inv = ROPE_THETA ** (-jnp.arange(0, ROT_DIM, 2, dtype=F32) / ROT_DIM)
    ang = pos.astype(F32)[:, None] * inv[None, :]
    cos, sin = jnp.cos(ang), jnp.sin(ang)
    n = pos.shape[0]
    pad = jnp.zeros((n, A_QK_DIM - ROT_DIM), F32)
    zeros = jnp.zeros((n, half), F32)
    c = jnp.concatenate([cos, cos, pad + 1.0], axis=1)
    s1 = jnp.concatenate([-sin, zeros, pad], axis=1)
    s2 = jnp.concatenate([zeros, sin, pad], axis=1)
    rep = LANES // A_QK_DIM
    return tuple(jnp.tile(a, (1, rep)) for a in (c, s1, s2))


def _group_mean_matrix():
    idx = np.arange(MXU_DIM) // A_QK_DIM
    return jnp.asarray((idx[:, None] == idx[None, :]).astype(np.float32) / A_QK_DIM, dtype=BF16)


def kernel(x_prompt, x_sample, cache_a_k, cache_a_v, cache_b_k, cache_b_v, cache_b_logf,
           c_prompt, c_sample, norm_g, w_ada, b_ada, w_in, b_f, qn_a, kn_a,
           lam_q1, lam_k1, lam_q2, lam_k2, subln_g, qn_b, kn_b, w_out):
    depth = norm_g.shape[0]
    bp, seq, d = x_prompt.shape
    bs, tdec, _ = x_sample.shape
    past = cache_a_k.shape[2]
    assert bp == 1 and seq % ATTN_TILE == 0 and ATTN_TILE % CHUNK == 0
    assert past % CHUNK == 0 and tdec == CHUNK and (bs * tdec) % PROJ_ROWS == 0

    xp = x_prompt.reshape(seq, d)
    xs = x_sample.reshape(bs * tdec, d)
    c_rows = bp + bs
    c_all = jnp.concatenate([c_prompt, c_sample, jnp.zeros((16 - c_rows, d), F32)], axis=0)
    rope_p = _rope_tables(jnp.arange(seq))
    rope_s = _rope_tables(jnp.tile(past + jnp.arange(tdec), bs))
    gm = _group_mean_matrix()
    fb_off = 2 * A_QK_WIDTH + 2 * A_WIDTH + 3 * B_WIDTH

    outs_p = [[] for _ in range(5)]
    outs_s = [[] for _ in range(5)]
    for l in range(depth):
        lam_init = 0.8 - 0.6 * math.exp(-0.3 * l)
        w_main = jnp.concatenate([w_in[l][:, :fb_off], w_in[l][:, fb_off + B_HEADS:]],
                                 axis=1).astype(BF16)
        wf_t = jnp.concatenate([w_in[l][:, fb_off:fb_off + B_HEADS].T,
                                jnp.zeros((16 - B_HEADS, d), F32)], axis=0).astype(BF16)
        bf_col = jnp.concatenate([b_f[l], jnp.zeros((16 - B_HEADS,), F32)]).reshape(16, 1)
        gains = tuple(jnp.tile(g[l], 512 // g.shape[1]).reshape(1, 512)
                      for g in (qn_a, kn_a, qn_b, kn_b))
        lam_vecs = tuple(v[l].reshape(1, A_QK_DIM) for v in (lam_q1, lam_k1, lam_q2, lam_k2))
        sg = subln_g[l].reshape(1, A_V_DIM)
        w_o = w_out[l].astype(BF16)
        g_row = norm_g[l].reshape(1, d)

        mod = _modulation(c_all, w_ada[l], b_ada[l])
        shift, scale, gate = mod[:, :d], mod[:, d:2 * d], mod[:, 2 * d:]

        (qa, ka, kab, va, vab, za, qb, kb, kbb, vb, vbaug, zb, lf_t, cum_t) = _projection(
            xp, scale[:bp], shift[:bp], g_row, w_main, wf_t, bf_col, gains, rope_p, gm)
        ga = _attn_a_prompt(qa, kab, vab, za, lam_vecs, sg, lam_init)
        gb = _attn_b_prompt(qb, kbb, vbaug, cum_t, zb)
        xp = _out_projection(xp, gate[:bp], ga, gb, w_o)
        for dst, a in zip(outs_p, (ka.reshape(bp, seq, A_HEADS, 2, A_QK_DIM),
                                   va.reshape(bp, seq, A_HEADS, A_V_DIM),
                                   kb.reshape(bp, seq, B_HEADS, B_DIM),
                                   vb.reshape(bp, seq, B_HEADS, B_DIM),
                                   lf_t.T.reshape(bp, seq, B_HEADS))):
            dst.append(a)

        rep = lambda a: jnp.repeat(a[bp:c_rows], tdec, axis=0)
        (qa, ka, kab, va, vab, za, qb, kb, kbb, vb, vbaug, zb, lf_t, _) = _projection(
            xs, rep(scale), rep(shift), g_row, w_main, wf_t, bf_col, gains, rope_s, gm)
        lf_new = lf_t.reshape(B_HEADS, bs, tdec).transpose(1, 0, 2)
        lf_cat = jnp.concatenate([cache_b_logf[l].astype(F32).transpose(0, 2, 1), lf_new,
                                  jnp.zeros((bs, B_HEADS, LANES - tdec), F32)], axis=2)
        caches = (cache_a_k[l].reshape(bs, past, A_QK_WIDTH), cache_a_v[l].reshape(bs, past, A_WIDTH),
                  cache_b_k[l].reshape(bs, past, B_WIDTH), cache_b_v[l].reshape(bs, past, B_WIDTH))
        ga, gb = _attn_decode((qa, kab, vab, za, qb, kbb, vbaug, zb), caches, lf_cat,
                              lam_vecs, sg, lam_init, bs, tdec)
        xs = _out_projection(xs, rep(gate), ga, gb, w_o)
        for dst, a in zip(outs_s, (ka.reshape(bs, tdec, A_HEADS, 2, A_QK_DIM),
                                   va.reshape(bs, tdec, A_HEADS, A_V_DIM),
                                   kb.reshape(bs, tdec, B_HEADS, B_DIM),
                                   vb.reshape(bs, tdec, B_HEADS, B_DIM),
                                   lf_t.T.reshape(bs, tdec, B_HEADS))):
            dst.append(a)

    return (xp.reshape(bp, seq, d), xs.reshape(bs, tdec, d),
            *(jnp.stack(o) for o in outs_p), *(jnp.stack(o) for o in outs_s))
```

```python
import functools
import math

import numpy as np
import jax
import jax.numpy as jnp
from jax import lax
from jax.experimental import pallas as pl
from jax.experimental.pallas import tpu as pltpu

F32 = jnp.float32
BF16 = jnp.bfloat16

CHUNK = 64
A_HEADS = 4
A_QK_DIM = 64
A_V_DIM = 2 * A_QK_DIM
B_HEADS = 8
B_DIM = 64
A_WIDTH = A_HEADS * A_V_DIM
B_WIDTH = B_HEADS * B_DIM
A_QK_WIDTH = A_HEADS * 2 * A_QK_DIM
ROT_DIM = A_QK_DIM // 4
ROPE_THETA = 500000.0
EPS = 1e-6

LANES = 128
BF16_ROWS = 16
MXU_DIM = 256
VMEM_LIMIT = 56 * 1024 * 1024

SLAB = MXU_DIM
PROJ_ROWS = SLAB
KV_TILE = 2 * SLAB
SCORE_COLS = 2 * SLAB
A_Q_TILE = SCORE_COLS // 2
B_Q_TILE = SCORE_COLS
OUT_ROWS = 512

A_VT_ROWS = A_V_DIM + BF16_ROWS
B_VT_ROWS = B_DIM + BF16_ROWS
B_BIAS_PIECES = 3

_NT = (((1,), (1,)), ((), ()))


def _params(n_axes):
    return pltpu.CompilerParams(dimension_semantics=("arbitrary",) * n_axes,
                                vmem_limit_bytes=VMEM_LIMIT)


def _silu(x):
    return x * jax.nn.sigmoid(x)


def _mod_kernel(c_ref, w_ref, b_ref, o_ref):
    a = _silu(c_ref[...]).astype(BF16)
    o_ref[...] = jnp.dot(a, w_ref[...].astype(BF16), preferred_element_type=F32) + b_ref[...]


def _modulation(c_all, w_ada, b_ada):
    rows, d = c_all.shape
    n = w_ada.shape[1]
    bn = 512
    return pl.pallas_call(
        _mod_kernel,
        grid=(n // bn,),
        in_specs=[pl.BlockSpec((rows, d), lambda j: (0, 0)),
                  pl.BlockSpec((d, bn), lambda j: (0, j)),
                  pl.BlockSpec((1, bn), lambda j: (0, j))],
        out_specs=pl.BlockSpec((rows, bn), lambda j: (0, j)),
        out_shape=jax.ShapeDtypeStruct((rows, n), F32),
        compiler_params=_params(1),
        name="modulation",
    )(c_all, w_ada, b_ada.reshape(1, n))


def _group_rms(u, gm):
    sq = u * u
    hi = sq.astype(BF16)
    lo = (sq - hi.astype(F32)).astype(BF16)
    parts = []
    for c in range(u.shape[1] // MXU_DIM):
        sl = slice(MXU_DIM * c, MXU_DIM * (c + 1))
        parts.append(jnp.dot(hi[:, sl], gm, preferred_element_type=F32)
                     + jnp.dot(lo[:, sl], gm, preferred_element_type=F32))
    ms = jnp.concatenate(parts, axis=1)
    return u * lax.rsqrt(ms + EPS)


def _rope(n, c, s1, s2):
    reps = n.shape[1] // LANES
    c = jnp.concatenate([c] * reps, axis=1)
    s1 = jnp.concatenate([s1] * reps, axis=1)
    s2 = jnp.concatenate([s2] * reps, axis=1)
    half = ROT_DIM // 2
    return (n * c + pltpu.roll(n, n.shape[1] - half, 1) * s1 + pltpu.roll(n, half, 1) * s2)


def _with_ones_rows(xt, rows, n_ones):
    r = lax.broadcasted_iota(jnp.int32, (rows, xt.shape[1]), 0)
    return jnp.concatenate([xt, jnp.where(r < n_ones, 1.0, 0.0)], axis=0)


def _proj_kernel(x_ref, scale_ref, shift_ref, g_ref, w_ref, wf_ref, bf_ref,
                 qna_ref, kna_ref, qnb_ref, knb_ref, rc_ref, rs1_ref, rs2_ref, gm_ref,
                 ka_o, va_o, za_o, kb_o, vb_o, zb_o, lf_o,
                 qa_o, kab_o, vab_o, qb_o, kbb_o, vbb_o, carry_ref, *, transposed):
    tm = x_ref.shape[0]
    x = x_ref[...]
    y = x * lax.rsqrt(jnp.mean(x * x, axis=-1, keepdims=True) + EPS) * g_ref[...]
    h = y * (1.0 + scale_ref[...]) + shift_ref[...]
    hb = h.astype(BF16)
    gm = gm_ref[...]
    rc, rs1, rs2 = rc_ref[...], rs1_ref[...], rs2_ref[...]
    qk_scale = A_QK_DIM ** -0.5
    lane = lax.broadcasted_iota(jnp.int32, (tm, LANES), 1)

    def group(i):
        return jnp.dot(hb, w_ref[:, 512 * i:512 * (i + 1)], preferred_element_type=F32)

    fb = lax.dot_general(wf_ref[...], hb, _NT, preferred_element_type=F32)
    fx = fb + bf_ref[...]
    logf = jnp.minimum(fx, 0.0) - jnp.log1p(jnp.exp(-jnp.abs(fx)))
    lf_o[...] = logf[:B_HEADS]

    qa = _rope(_group_rms(group(0), gm) * qna_ref[...], rc, rs1, rs2) * qk_scale
    ka = _rope(_group_rms(group(1), gm) * kna_ref[...], rc, rs1, rs2)
    ka_o[...] = ka
    kab_o[...] = ka.astype(BF16)
    va = group(2)
    va_o[...] = va
    za_o[...] = group(3)
    qb = _group_rms(group(4), gm) * qnb_ref[...] * qk_scale
    kb = _group_rms(group(5), gm) * knb_ref[...]
    kb_o[...] = kb
    vb = group(6)
    vb_o[...] = vb
    zb_o[...] = group(7)

    if not transposed:
        qa_o[...] = qa.astype(BF16)
        vab_o[...] = va.astype(BF16)
        qb_o[...] = qb.astype(BF16)
        kbb_o[...] = kb.astype(BF16)
        for p in range(B_HEADS // 2):
            blk = vb[:, LANES * p:LANES * (p + 1)]
            even = jnp.where(lane < B_DIM, blk, jnp.where(lane == B_DIM, 1.0, 0.0))
            odd = jnp.where(lane >= B_DIM, blk, jnp.where(lane == 0, 1.0, 0.0))
            vbb_o[:, 2 * LANES * p:2 * LANES * p + LANES] = even.astype(BF16)
            vbb_o[:, 2 * LANES * p + LANES:2 * LANES * (p + 1)] = odd.astype(BF16)
        return

    qa_o[...] = qa.T.astype(BF16)
    va_t = va.T
    for hd in range(A_HEADS):
        blk = _with_ones_rows(va_t[A_V_DIM * hd:A_V_DIM * (hd + 1)], BF16_ROWS, 1)
        vab_o[0, A_VT_ROWS * hd:A_VT_ROWS * (hd + 1), :] = blk.astype(BF16)
    qb_t = qb.T
    vb_t = vb.T
    for hd in range(B_HEADS):
        blk = _with_ones_rows(qb_t[B_DIM * hd:B_DIM * (hd + 1)], LANES - B_DIM, B_BIAS_PIECES)
        qb_o[LANES * hd:LANES * (hd + 1), :] = blk.astype(BF16)
        blk = _with_ones_rows(vb_t[B_DIM * hd:B_DIM * (hd + 1)], BF16_ROWS, 1)
        vbb_o[0, B_VT_ROWS * hd:B_VT_ROWS * (hd + 1), :] = blk.astype(BF16)

    @pl.when(pl.program_id(0) == 0)
    def _():
        carry_ref[...] = jnp.zeros_like(carry_ref)

    pos = lax.broadcasted_iota(jnp.int32, logf.shape, 1)
    c = logf
    s = 1
    while s < tm:
        c = c + jnp.where(pos >= s, pltpu.roll(c, s, 1), 0.0)
        s *= 2
    c = c + carry_ref[...]
    carry_ref[...] = c[:, tm - 1:tm]

    cum_rows = jnp.concatenate([c, jnp.zeros((LANES - c.shape[0], tm), F32)], axis=0).T
    for hd in range(B_HEADS):
        blk = kb[:, LANES * (hd // 2):LANES * (hd // 2 + 1)]
        if hd % 2:
            blk = pltpu.roll(blk, B_DIM, 1)
        neg = -jnp.broadcast_to(cum_rows[:, hd:hd + 1], (tm, LANES))
        hi = neg.astype(BF16).astype(F32)
        rest = neg - hi
        mid = rest.astype(BF16).astype(F32)
        lo = rest - mid
        aug = jnp.where(lane < B_DIM, blk,
                        jnp.where(lane == B_DIM, hi,
                                  jnp.where(lane == B_DIM + 1, mid,
                                            jnp.where(lane == B_DIM + 2, lo, 0.0))))
        kbb_o[:, LANES * hd:LANES * (hd + 1)] = aug.astype(BF16)


def _projection(x2, scale, shift, norm_g, w_main, wf_t, bf_col, gains, rope_tabs, gm, transposed):
    t, d = x2.shape
    tm = PROJ_ROWS
    per_row = scale.shape[0] != 1
    mod_spec = (pl.BlockSpec((tm, d), lambda i: (i, 0)) if per_row
                else pl.BlockSpec((1, d), lambda i: (0, 0)))
    const = lambda shape: pl.BlockSpec(shape, lambda i: (0, 0))
    row = lambda width: pl.BlockSpec((tm, width), lambda i: (i, 0))
    col = lambda height: pl.BlockSpec((height, tm), lambda i: (0, i))
    slab = lambda height: pl.BlockSpec((1, height, tm), lambda i: (i, 0, 0))
    in_specs = [row(d), mod_spec, mod_spec, const((1, d)),
                const(w_main.shape), const(wf_t.shape), const(bf_col.shape),
                const((1, 512)), const((1, 512)), const((1, 512)), const((1, 512)),
                row(LANES), row(LANES), row(LANES), const(gm.shape)]
    f32o = jax.ShapeDtypeStruct((t, 512), F32)
    bf = lambda *shape: jax.ShapeDtypeStruct(shape, BF16)
    out_shape = [f32o] * 6 + [jax.ShapeDtypeStruct((B_HEADS, t), F32)]
    out_specs = [row(512)] * 6 + [col(B_HEADS)]
    if transposed:
        out_shape += [bf(512, t), bf(t, 512), bf(t // tm, A_HEADS * A_VT_ROWS, tm),
                      bf(B_HEADS * LANES, t), bf(t, B_HEADS * LANES),
                      bf(t // tm, B_HEADS * B_VT_ROWS, tm)]
        out_specs += [col(512), row(512), slab(A_HEADS * A_VT_ROWS),
                      col(B_HEADS * LANES), row(B_HEADS * LANES), slab(B_HEADS * B_VT_ROWS)]
    else:
        out_shape += [bf(t, 512)] * 5 + [bf(t, 1024)]
        out_specs += [row(512)] * 5 + [row(1024)]
    return pl.pallas_call(
        functools.partial(_proj_kernel, transposed=transposed),
        grid=(t // tm,),
        in_specs=in_specs,
        out_specs=out_specs,
        out_shape=out_shape,
        scratch_shapes=[pltpu.VMEM((16, 1), F32)],
        compiler_params=_params(1),
        name="projection_prompt" if transposed else "projection_decode",
    )(x2, scale, shift, norm_g, w_main, wf_t, bf_col, *gains, *rope_tabs, gm)


def _col_max(s):
    groups = s.reshape(8, s.shape[0] // 8, s.shape[1])
    return jnp.max(jnp.max(groups, axis=0), axis=0, keepdims=True)


def _kv_sweep(n_full, q_cols, k_of, v_of, last_mask, s_sc, mt_sc, m_sc, acc_sc):
    def produce(j, slot):
        s = jnp.dot(k_of(j), q_cols, preferred_element_type=F32)
        s_sc[slot] = s
        mt_sc[slot] = _col_max(s)

    def consume(j, slot, mask):
        s = s_sc[slot]
        if mask is None:
            tile_max = mt_sc[slot]
        else:
            s = jnp.where(mask, s, -jnp.inf)
            tile_max = _col_max(s)
        m_prev = m_sc[...]
        m_new = jnp.maximum(m_prev, tile_max)
        p = jnp.exp(s - m_new).astype(BF16)
        pv = None
        for c in range(KV_TILE // SLAB):
            part = jnp.dot(v_of(j * (KV_TILE // SLAB) + c), p[SLAB * c:SLAB * (c + 1)],
                           preferred_element_type=F32)
            pv = part if pv is None else pv + part
        acc_sc[...] = jnp.exp(m_prev - m_new) * acc_sc[...] + pv
        m_sc[...] = m_new

    m_sc[...] = jnp.full(m_sc.shape, -jnp.inf, F32)
    acc_sc[...] = jnp.zeros_like(acc_sc)
    produce(0, 0)

    def body(i, carry):
        produce(2 * i + 1, 1)
        consume(2 * i, 0, None)
        produce(2 * i + 2, 0)
        consume(2 * i + 1, 1, None)
        return carry

    lax.fori_loop(0, n_full // 2, body, 0)

    @pl.when(n_full % 2 == 1)
    def _():
        produce(n_full, 1)
        consume(n_full - 1, 0, None)
        consume(n_full, 1, last_mask)

    @pl.when(n_full % 2 == 0)
    def _():
        consume(n_full, 0, last_mask)


def _lam(lq1_ref, lk1_ref, lq2_ref, lk2_ref, lam_init):
    return (jnp.exp(jnp.sum(lq1_ref[...] * lk1_ref[...], axis=1, keepdims=True))
            - jnp.exp(jnp.sum(lq2_ref[...] * lk2_ref[...], axis=1, keepdims=True)) + lam_init)


def _attn_a_kernel(qt_ref, k_ref, vt_ref, z_ref, lq1_ref, lk1_ref, lq2_ref, lk2_ref, sg_ref,
                   o_ref, s_sc, mt_sc, m_sc, acc_sc, *, lam_init):
    tq = A_Q_TILE
    qi = pl.program_id(1)
    qt = qt_ref[...]
    row = lax.broadcasted_iota(jnp.int32, qt.shape, 0)
    zero = jnp.zeros_like(qt)
    q_cols = jnp.concatenate([jnp.where(row < A_QK_DIM, qt, zero),
                              jnp.where(row >= A_QK_DIM, qt, zero)], axis=1)
    n_full = (qi * tq) // KV_TILE
    shape = (KV_TILE, SCORE_COLS)
    k_chunk = lax.broadcasted_iota(jnp.int32, shape, 0) // CHUNK + n_full * (KV_TILE // CHUNK)
    q_chunk = ((lax.broadcasted_iota(jnp.int32, shape, 1) % tq) // CHUNK + qi * (tq // CHUNK))

    _kv_sweep(n_full, q_cols,
              lambda j: k_ref[pl.ds(pl.multiple_of(j * KV_TILE, KV_TILE), KV_TILE), :],
              lambda sl: vt_ref[sl], k_chunk <= q_chunk, s_sc, mt_sc, m_sc, acc_sc)

    lam = _lam(lq1_ref, lk1_ref, lq2_ref, lk2_ref, lam_init)
    acc = acc_sc[...]
    o = acc[:A_V_DIM] / acc[A_V_DIM:A_V_DIM + 1]
    oa = o[:, :tq] - lam * o[:, tq:]
    on = oa * lax.rsqrt(jnp.mean(oa * oa, axis=0, keepdims=True) + EPS)
    on = on * (sg_ref[...] * (1.0 - lam_init))
    o_ref[...] = (on.T * _silu(z_ref[...])).astype(o_ref.dtype)


def _attn_a_prompt(qa_t, kab, va_t, za, lam_vecs, subln_col, lam_init):
    t = kab.shape[0]
    tq = A_Q_TILE
    vec = pl.BlockSpec((1, A_QK_DIM), lambda h, i: (0, 0))
    return pl.pallas_call(
        functools.partial(_attn_a_kernel, lam_init=lam_init),
        grid=(A_HEADS, t // tq),
        in_specs=[pl.BlockSpec((LANES, tq), lambda h, i: (h, i)),
                  pl.BlockSpec((t, LANES), lambda h, i: (0, h)),
                  pl.BlockSpec((t // SLAB, A_VT_ROWS, SLAB), lambda h, i: (0, h, 0)),
                  pl.BlockSpec((tq, LANES), lambda h, i: (i, h)),
                  vec, vec, vec, vec,
                  pl.BlockSpec((A_V_DIM, 1), lambda h, i: (0, 0))],
        out_specs=pl.BlockSpec((tq, LANES), lambda h, i: (i, h)),
        out_shape=jax.ShapeDtypeStruct((t, A_WIDTH), BF16),
        scratch_shapes=[pltpu.VMEM((2, KV_TILE, SCORE_COLS), F32),
                        pltpu.VMEM((2, 1, SCORE_COLS), F32), pltpu.VMEM((1, SCORE_COLS), F32),
                        pltpu.VMEM((A_VT_ROWS, SCORE_COLS), F32)],
        compiler_params=_params(2),
        name="attn_a_prompt",
    )(qa_t, kab, va_t, za, *lam_vecs, subln_col)


def _attn_b_kernel(qt_ref, k_ref, vt_ref, z_ref, o_ref, s_sc, mt_sc, m_sc, acc_sc):
    qi = pl.program_id(1)
    shape = (KV_TILE, SCORE_COLS)
    causal = lax.broadcasted_iota(jnp.int32, shape, 0) <= lax.broadcasted_iota(jnp.int32, shape, 1)
    outs = []
    for t in range(2):
        _kv_sweep(qi, qt_ref[LANES * t:LANES * (t + 1), :],
                  lambda j: k_ref[pl.ds(pl.multiple_of(j * KV_TILE, KV_TILE), KV_TILE),
                                  LANES * t:LANES * (t + 1)],
                  lambda sl: vt_ref[sl, B_VT_ROWS * t:B_VT_ROWS * (t + 1), :],
                  causal, s_sc, mt_sc, m_sc, acc_sc)
        acc = acc_sc[...]
        outs.append(acc[:B_DIM] / acc[B_DIM:B_DIM + 1])
    o = jnp.concatenate(outs, axis=0)
    o_ref[...] = (o.T * _silu(z_ref[...])).astype(o_ref.dtype)


def _attn_b_prompt(qb_t, kb_aug, vb_t, zb):
    t = kb_aug.shape[0]
    tq = B_Q_TILE
    assert tq == KV_TILE
    return pl.pallas_call(
        _attn_b_kernel,
        grid=(B_HEADS // 2, t // tq),
        in_specs=[pl.BlockSpec((2 * LANES, tq), lambda p, i: (p, i)),
                  pl.BlockSpec((t, 2 * LANES), lambda p, i: (0, p)),
                  pl.BlockSpec((t // SLAB, 2 * B_VT_ROWS, SLAB), lambda p, i: (0, p, 0)),
                  pl.BlockSpec((tq, LANES), lambda p, i: (i, p))],
        out_specs=pl.BlockSpec((tq, LANES), lambda p, i: (i, p)),
        out_shape=jax.ShapeDtypeStruct((t, B_WIDTH), BF16),
        scratch_shapes=[pltpu.VMEM((2, KV_TILE, SCORE_COLS), F32),
                        pltpu.VMEM((2, 1, SCORE_COLS), F32), pltpu.VMEM((1, SCORE_COLS), F32),
                        pltpu.VMEM((B_VT_ROWS, SCORE_COLS), F32)],
        compiler_params=_params(2),
        name="attn_b_prompt",
    )(qb_t, kb_aug, vb_t, zb)


def _softmax_two_part(s_c, s_n, v_c, v_n, want_l):
    m = jnp.maximum(jnp.max(s_c, axis=1, keepdims=True), jnp.max(s_n, axis=1, keepdims=True))
    p_c = jnp.exp(s_c - m)
    p_n = jnp.exp(s_n - m)
    acc = (jnp.dot(p_c.astype(BF16), v_c, preferred_element_type=F32)
           + jnp.dot(p_n.astype(BF16), v_n, preferred_element_type=F32))
    l = None
    if want_l:
        l = jnp.sum(p_c, axis=1, keepdims=True) + jnp.sum(p_n, axis=1, keepdims=True)
    return acc, l


def _diff_epilogue(o0, o1, lam, subln_g, lam_init, z):
    oa = o0 - lam * o1
    on = oa * lax.rsqrt(jnp.mean(oa * oa, axis=-1, keepdims=True) + EPS)
    return on * subln_g * (1.0 - lam_init) * _silu(z)


def _fox_epilogue(acc0, acc1, z):
    lane = lax.broadcasted_iota(jnp.int32, acc0.shape, 1)
    o = jnp.where(lane < B_DIM, acc0 / acc0[:, B_DIM:B_DIM + 1], acc1 / acc1[:, 0:1])
    return o * _silu(z)


def _attn_dec_kernel(qa_ref, kan_ref, van_ref, za_ref, qb_ref, kbn_ref, vbn_ref, zb_ref,
                     cak_ref, cav_ref, cbk_ref, cbv_ref, lf_ref,
                     lq1_ref, lk1_ref, lq2_ref, lk2_ref, sg_ref,
                     ga_o, gb_o, *, lam_init):
    tdec = qa_ref.shape[0]
    past = cak_ref.shape[1]
    lane = lax.broadcasted_iota(jnp.int32, (tdec, LANES), 1)
    lam = _lam(lq1_ref, lk1_ref, lq2_ref, lk2_ref, lam_init)

    for h in range(A_HEADS):
        cols = slice(LANES * h, LANES * (h + 1))
        q = qa_ref[:, cols]
        zero = jnp.zeros_like(q)
        k_c = cak_ref[0, :, cols].astype(BF16)
        v_c = cav_ref[0, :, cols].astype(BF16)
        k_n = kan_ref[:, cols]
        v_n = van_ref[:, cols]
        outs = []
        for t in range(2):
            qm = jnp.where((lane >= A_QK_DIM) == (t == 1), q, zero)
            s_c = lax.dot_general(qm, k_c, _NT, preferred_element_type=F32)
            s_n = lax.dot_general(qm, k_n, _NT, preferred_element_type=F32)
            acc, l = _softmax_two_part(s_c, s_n, v_c, v_n, True)
            outs.append(acc / l)
        ga_o[:, cols] = _diff_epilogue(outs[0], outs[1], lam, sg_ref[...], lam_init,
                                       za_ref[:, cols]).astype(ga_o.dtype)

    lf = lf_ref[0]
    pos = lax.broadcasted_iota(jnp.int32, lf.shape, 1)
    cum = lf
    s = 1
    while s < lf.shape[1]:
        cum = cum + jnp.where(pos >= s, pltpu.roll(cum, s, 1), 0.0)
        s *= 2
    row = lax.broadcasted_iota(jnp.int32, (tdec, tdec), 0)
    col = lax.broadcasted_iota(jnp.int32, (tdec, tdec), 1)
    lane_c = lax.broadcasted_iota(jnp.int32, (past, LANES), 1)
    for p in range(B_HEADS // 2):
        cols = slice(LANES * p, LANES * (p + 1))
        q = qb_ref[:, cols]
        zero = jnp.zeros_like(q)
        k_c = cbk_ref[0, :, cols].astype(BF16)
        k_n = kbn_ref[:, cols]
        v_c = cbv_ref[0, :, cols]
        accs = []
        for t in range(2):
            qm = jnp.where((lane >= B_DIM) == (t == 1), q, zero)
            if t == 0:
                v_aug = jnp.where(lane_c < B_DIM, v_c, jnp.where(lane_c == B_DIM, 1.0, 0.0))
            else:
                v_aug = jnp.where(lane_c >= B_DIM, v_c, jnp.where(lane_c == 0, 1.0, 0.0))
            head = 2 * p + t
            s_c = lax.dot_general(qm, k_c, _NT, preferred_element_type=F32)
            s_c = s_c - cum[head:head + 1, :past]
            s_n = lax.dot_general(qm, k_n, _NT, preferred_element_type=F32)
            s_n = s_n - cum[head:head + 1, past:past + tdec]
            s_n = jnp.where(col <= row, s_n, -jnp.inf)
            v_n = vbn_ref[:, LANES * head:LANES * (head + 1)]
            acc, _ = _softmax_two_part(s_c, s_n, v_aug.astype(BF16), v_n, False)
            accs.append(acc)
        gb_o[:, cols] = _fox_epilogue(accs[0], accs[1], zb_ref[:, cols]).astype(gb_o.dtype)


def _attn_decode(proj, caches, lf_cat, lam_vecs, subln_g, lam_init, nb, tdec):
    qa, kab, vab, za, qb, kbb, vbaug, zb = proj
    cak, cav, cbk, cbv = caches
    past = cak.shape[1]
    rows = lambda w: pl.BlockSpec((tdec, w), lambda b: (b, 0))
    cache = pl.BlockSpec((1, past, 512), lambda b: (b, 0, 0))
    vec = pl.BlockSpec((1, A_QK_DIM), lambda b: (0, 0))
    return pl.pallas_call(
        functools.partial(_attn_dec_kernel, lam_init=lam_init),
        grid=(nb,),
        in_specs=[rows(512), rows(512), rows(512), rows(512),
                  rows(512), rows(512), rows(1024), rows(512),
                  cache, cache, cache, cache,
                  pl.BlockSpec((1, B_HEADS, lf_cat.shape[2]), lambda b: (b, 0, 0)),
                  vec, vec, vec, vec, pl.BlockSpec((1, A_V_DIM), lambda b: (0, 0))],
        out_specs=[rows(512), rows(512)],
        out_shape=[jax.ShapeDtypeStruct((nb * tdec, A_WIDTH), BF16),
                   jax.ShapeDtypeStruct((nb * tdec, B_WIDTH), BF16)],
        compiler_params=_params(1),
        name="attn_decode",
    )(qa, kab, vab, za, qb, kbb, vbaug, zb, cak, cav, cbk, cbv, lf_cat, *lam_vecs, subln_g)


def _out_kernel(x_ref, gate_ref, ga_ref, gb_ref, w_ref, y_ref):
    o = (jnp.dot(ga_ref[...], w_ref[:A_WIDTH, :], preferred_element_type=F32)
         + jnp.dot(gb_ref[...], w_ref[A_WIDTH:, :], preferred_element_type=F32))
    y_ref[...] = x_ref[...] + gate_ref[...] * o


def _out_projection(x2, gate, ga, gb, w_out):
    t, d = x2.shape
    tm = OUT_ROWS
    gate_spec = (pl.BlockSpec((tm, d), lambda i: (i, 0)) if gate.shape[0] != 1
                 else pl.BlockSpec((1, d), lambda i: (0, 0)))
    return pl.pallas_call(
        _out_kernel,
        grid=(t // tm,),
        in_specs=[pl.BlockSpec((tm, d), lambda i: (i, 0)), gate_spec,
                  pl.BlockSpec((tm, A_WIDTH), lambda i: (i, 0)),
                  pl.BlockSpec((tm, B_WIDTH), lambda i: (i, 0)),
                  pl.BlockSpec(w_out.shape, lambda i: (0, 0))],
        out_specs=pl.BlockSpec((tm, d), lambda i: (i, 0)),
        out_shape=jax.ShapeDtypeStruct((t, d), F32),
        compiler_params=_params(1),
        name="out_projection",
    )(x2, gate, ga, gb, w_out)


def _rope_tables(pos):
    half = ROT_DIM // 2
    inv = ROPE_THETA ** (-jnp.arange(0, ROT_DIM, 2, dtype=F32) / ROT_DIM)
    ang = pos.astype(F32)[:, None] * inv[None, :]
    cos, sin = jnp.cos(ang), jnp.sin(ang)
    n = pos.shape[0]
    pad = jnp.zeros((n, A_QK_DIM - ROT_DIM), F32)
    zeros = jnp.zeros((n, half), F32)
    c = jnp.concatenate([cos, cos, pad + 1.0], axis=1)
    s1 = jnp.concatenate([-sin, zeros, pad], axis=1)
    s2 = jnp.concatenate([zeros, sin, pad], axis=1)
    rep = LANES // A_QK_DIM
    return tuple(jnp.tile(a, (1, rep)) for a in (c, s1, s2))


def _group_mean_matrix():
    idx = np.arange(MXU_DIM) // A_QK_DIM
    return jnp.asarray((idx[:, None] == idx[None, :]).astype(np.float32) / A_QK_DIM, dtype=BF16)


def kernel(x_prompt, x_sample, cache_a_k, cache_a_v, cache_b_k, cache_b_v, cache_b_logf,
           c_prompt, c_sample, norm_g, w_ada, b_ada, w_in, b_f, qn_a, kn_a,
           lam_q1, lam_k1, lam_q2, lam_k2, subln_g, qn_b, kn_b, w_out):
    depth = norm_g.shape[0]
    bp, seq, d = x_prompt.shape
    bs, tdec, _ = x_sample.shape
    past = cache_a_k.shape[2]
    assert bp == 1 and seq % KV_TILE == 0 and A_Q_TILE % CHUNK == 0 and seq % OUT_ROWS == 0
    assert past % CHUNK == 0 and tdec == CHUNK and (bs * tdec) % PROJ_ROWS == 0

    xp = x_prompt.reshape(seq, d)
    xs = x_sample.reshape(bs * tdec, d)
    c_rows = bp + bs
    c_all = jnp.concatenate([c_prompt, c_sample, jnp.zeros((16 - c_rows, d), F32)], axis=0)
    rope_p = _rope_tables(jnp.arange(seq))
    rope_s = _rope_tables(jnp.tile(past + jnp.arange(tdec), bs))
    gm = _group_mean_matrix()
    fb_off = 2 * A_QK_WIDTH + 2 * A_WIDTH + 3 * B_WIDTH

    outs_p = [[] for _ in range(5)]
    outs_s = [[] for _ in range(5)]
    for l in range(depth):
        lam_init = 0.8 - 0.6 * math.exp(-0.3 * l)
        w_main = jnp.concatenate([w_in[l][:, :fb_off], w_in[l][:, fb_off + B_HEADS:]],
                                 axis=1).astype(BF16)
        wf_t = jnp.concatenate([w_in[l][:, fb_off:fb_off + B_HEADS].T,
                                jnp.zeros((16 - B_HEADS, d), F32)], axis=0).astype(BF16)
        bf_col = jnp.concatenate([b_f[l], jnp.zeros((16 - B_HEADS,), F32)]).reshape(16, 1)
        gains = tuple(jnp.tile(g[l], 512 // g.shape[1]).reshape(1, 512)
                      for g in (qn_a, kn_a, qn_b, kn_b))
        lam_vecs = tuple(v[l].reshape(1, A_QK_DIM) for v in (lam_q1, lam_k1, lam_q2, lam_k2))
        w_o = w_out[l].astype(BF16)
        g_row = norm_g[l].reshape(1, d)

        mod = _modulation(c_all, w_ada[l], b_ada[l])
        shift, scale, gate = mod[:, :d], mod[:, d:2 * d], mod[:, 2 * d:]

        (ka, va, za, kb, vb, zb, lf_t, qa_t, kab, va_t, qb_t, kb_aug, vb_t) = _projection(
            xp, scale[:bp], shift[:bp], g_row, w_main, wf_t, bf_col, gains, rope_p, gm, True)
        ga = _attn_a_prompt(qa_t, kab, va_t, za, lam_vecs, subln_g[l].reshape(A_V_DIM, 1), lam_init)
        gb = _attn_b_prompt(qb_t, kb_aug, vb_t, zb)
        xp = _out_projection(xp, gate[:bp], ga, gb, w_o)
        for dst, a in zip(outs_p, (ka.reshape(bp, seq, A_HEADS, 2, A_QK_DIM),
                                   va.reshape(bp, seq, A_HEADS, A_V_DIM),
                                   kb.reshape(bp, seq, B_HEADS, B_DIM),
                                   vb.reshape(bp, seq, B_HEADS, B_DIM),
                                   lf_t.T.reshape(bp, seq, B_HEADS))):
            dst.append(a)

        rep = lambda a: jnp.repeat(a[bp:c_rows], tdec, axis=0)
        (ka, va, za, kb, vb, zb, lf_t, qa, kab, vab, qb, kbb, vbaug) = _projection(
            xs, rep(scale), rep(shift), g_row, w_main, wf_t, bf_col, gains, rope_s, gm, False)
        lf_new = lf_t.reshape(B_HEADS, bs, tdec).transpose(1, 0, 2)
        lf_cat = jnp.concatenate([cache_b_logf[l].astype(F32).transpose(0, 2, 1), lf_new,
                                  jnp.zeros((bs, B_HEADS, LANES - tdec), F32)], axis=2)
        caches = (cache_a_k[l].reshape(bs, past, A_QK_WIDTH), cache_a_v[l].reshape(bs, past, A_WIDTH),
                  cache_b_k[l].reshape(bs, past, B_WIDTH), cache_b_v[l].reshape(bs, past, B_WIDTH))
        ga, gb = _attn_decode((qa, kab, vab, za, qb, kbb, vbaug, zb), caches, lf_cat,
                              lam_vecs, subln_g[l].reshape(1, A_V_DIM), lam_init, bs, tdec)
        xs = _out_projection(xs, rep(gate), ga, gb, w_o)
        for dst, a in zip(outs_s, (ka.reshape(bs, tdec, A_HEADS, 2, A_QK_DIM),
                                   va.reshape(bs, tdec, A_HEADS, A_V_DIM),
                                   kb.reshape(bs, tdec, B_HEADS, B_DIM),
                                   vb.reshape(bs, tdec, B_HEADS, B_DIM),
                                   lf_t.T.reshape(bs, tdec, B_HEADS))):
            dst.append(a)

    return (xp.reshape(bp, seq, d), xs.reshape(bs, tdec, d),
            *(jnp.stack(o) for o in outs_p), *(jnp.stack(o) for o in outs_s))
```

```python
import functools
import math

import numpy as np
import jax
import jax.numpy as jnp
from jax import lax
from jax.experimental import pallas as pl
from jax.experimental.pallas import tpu as pltpu

F32 = jnp.float32
BF16 = jnp.bfloat16

CHUNK = 64
A_HEADS = 4
A_QK_DIM = 64
A_V_DIM = 2 * A_QK_DIM
B_HEADS = 8
B_DIM = 64
A_WIDTH = A_HEADS * A_V_DIM
B_WIDTH = B_HEADS * B_DIM
A_QK_WIDTH = A_HEADS * 2 * A_QK_DIM
ROT_DIM = A_QK_DIM // 4
ROPE_THETA = 500000.0
EPS = 1e-6
LOG2_E = math.log2(math.e)

LANES = 128
BF16_ROWS = 16
MXU_DIM = 256
VMEM_LIMIT = 56 * 1024 * 1024

SLAB = MXU_DIM
PROJ_ROWS = SLAB
KV_TILE = 2 * SLAB
Q_TILE = KV_TILE
STEP_STREAMS = 2
A_STEP_HEADS = STEP_STREAMS // 2
B_STEP_HEADS = STEP_STREAMS
OUT_ROWS = 512

A_VT_ROWS = A_V_DIM + BF16_ROWS
B_VT_ROWS = B_DIM + BF16_ROWS
B_BIAS_PIECES = 3

_NT = (((1,), (1,)), ((), ()))


def _params(n_axes, flags=None):
    return pltpu.CompilerParams(dimension_semantics=("arbitrary",) * n_axes,
                                vmem_limit_bytes=VMEM_LIMIT, flags=flags)


def _silu(x):
    return x * jax.nn.sigmoid(x)


def _mod_kernel(c_ref, w_ref, b_ref, o_ref):
    a = _silu(c_ref[...]).astype(BF16)
    o_ref[...] = jnp.dot(a, w_ref[...].astype(BF16), preferred_element_type=F32) + b_ref[...]


def _modulation(c_all, w_ada, b_ada):
    rows, d = c_all.shape
    n = w_ada.shape[1]
    bn = 512
    return pl.pallas_call(
        _mod_kernel,
        grid=(n // bn,),
        in_specs=[pl.BlockSpec((rows, d), lambda j: (0, 0)),
                  pl.BlockSpec((d, bn), lambda j: (0, j)),
                  pl.BlockSpec((1, bn), lambda j: (0, j))],
        out_specs=pl.BlockSpec((rows, bn), lambda j: (0, j)),
        out_shape=jax.ShapeDtypeStruct((rows, n), F32),
        compiler_params=_params(1),
        name="modulation",
    )(c_all, w_ada, b_ada.reshape(1, n))


def _group_rms(u, gm):
    sq = u * u
    hi = sq.astype(BF16)
    lo = (sq - hi.astype(F32)).astype(BF16)
    parts = []
    for c in range(u.shape[1] // MXU_DIM):
        sl = slice(MXU_DIM * c, MXU_DIM * (c + 1))
        parts.append(jnp.dot(hi[:, sl], gm, preferred_element_type=F32)
                     + jnp.dot(lo[:, sl], gm, preferred_element_type=F32))
    ms = jnp.concatenate(parts, axis=1)
    return u * lax.rsqrt(ms + EPS)


def _rope(n, c, s1, s2):
    reps = n.shape[1] // LANES
    c = jnp.concatenate([c] * reps, axis=1)
    s1 = jnp.concatenate([s1] * reps, axis=1)
    s2 = jnp.concatenate([s2] * reps, axis=1)
    half = ROT_DIM // 2
    return (n * c + pltpu.roll(n, n.shape[1] - half, 1) * s1 + pltpu.roll(n, half, 1) * s2)


def _with_ones_rows(xt, rows, n_ones):
    r = lax.broadcasted_iota(jnp.int32, (rows, xt.shape[1]), 0)
    return jnp.concatenate([xt, jnp.where(r < n_ones, 1.0, 0.0)], axis=0)


def _proj_kernel(x_ref, scale_ref, shift_ref, g_ref, w_ref, wf_ref, bf_ref,
                 qna_ref, kna_ref, qnb_ref, knb_ref, rc_ref, rs1_ref, rs2_ref, gm_ref,
                 ka_o, va_o, za_o, kb_o, vb_o, zb_o, lf_o,
                 qa_o, kab_o, vab_o, qb_o, kbb_o, vbb_o, carry_ref, *, transposed):
    tm = x_ref.shape[0]
    x = x_ref[...]
    y = x * lax.rsqrt(jnp.mean(x * x, axis=-1, keepdims=True) + EPS) * g_ref[...]
    h = y * (1.0 + scale_ref[...]) + shift_ref[...]
    hb = h.astype(BF16)
    gm = gm_ref[...]
    rc, rs1, rs2 = rc_ref[...], rs1_ref[...], rs2_ref[...]
    qk_scale = A_QK_DIM ** -0.5 * (LOG2_E if transposed else 1.0)
    lane = lax.broadcasted_iota(jnp.int32, (tm, LANES), 1)

    def group(i):
        return jnp.dot(hb, w_ref[:, 512 * i:512 * (i + 1)], preferred_element_type=F32)

    fb = lax.dot_general(wf_ref[...], hb, _NT, preferred_element_type=F32)
    fx = fb + bf_ref[...]
    logf = jnp.minimum(fx, 0.0) - jnp.log1p(jnp.exp(-jnp.abs(fx)))
    lf_o[...] = logf[:B_HEADS]

    qa = _rope(_group_rms(group(0), gm) * qna_ref[...], rc, rs1, rs2) * qk_scale
    ka = _rope(_group_rms(group(1), gm) * kna_ref[...], rc, rs1, rs2)
    ka_o[...] = ka
    kab_o[...] = ka.astype(BF16)
    va = group(2)
    va_o[...] = va
    za_o[...] = group(3)
    qb = _group_rms(group(4), gm) * qnb_ref[...] * qk_scale
    kb = _group_rms(group(5), gm) * knb_ref[...]
    kb_o[...] = kb
    vb = group(6)
    vb_o[...] = vb
    zb_o[...] = group(7)

    if not transposed:
        qa_o[...] = qa.astype(BF16)
        vab_o[...] = va.astype(BF16)
        qb_o[...] = qb.astype(BF16)
        kbb_o[...] = kb.astype(BF16)
        for p in range(B_HEADS // 2):
            blk = vb[:, LANES * p:LANES * (p + 1)]
            even = jnp.where(lane < B_DIM, blk, jnp.where(lane == B_DIM, 1.0, 0.0))
            odd = jnp.where(lane >= B_DIM, blk, jnp.where(lane == 0, 1.0, 0.0))
            vbb_o[:, 2 * LANES * p:2 * LANES * p + LANES] = even.astype(BF16)
            vbb_o[:, 2 * LANES * p + LANES:2 * LANES * (p + 1)] = odd.astype(BF16)
        return

    qa_o[...] = qa.T.astype(BF16)
    va_t = va.T
    for hd in range(A_HEADS):
        blk = _with_ones_rows(va_t[A_V_DIM * hd:A_V_DIM * (hd + 1)], BF16_ROWS, 1)
        vab_o[0, A_VT_ROWS * hd:A_VT_ROWS * (hd + 1), :] = blk.astype(BF16)
    qb_t = qb.T
    vb_t = vb.T
    for hd in range(B_HEADS):
        blk = _with_ones_rows(qb_t[B_DIM * hd:B_DIM * (hd + 1)], LANES - B_DIM, B_BIAS_PIECES)
        qb_o[LANES * hd:LANES * (hd + 1), :] = blk.astype(BF16)
        blk = _with_ones_rows(vb_t[B_DIM * hd:B_DIM * (hd + 1)], BF16_ROWS, 1)
        vbb_o[0, B_VT_ROWS * hd:B_VT_ROWS * (hd + 1), :] = blk.astype(BF16)

    @pl.when(pl.program_id(0) == 0)
    def _():
        carry_ref[...] = jnp.zeros_like(carry_ref)

    pos = lax.broadcasted_iota(jnp.int32, logf.shape, 1)
    c = logf
    s = 1
    while s < tm:
        c = c + jnp.where(pos >= s, pltpu.roll(c, s, 1), 0.0)
        s *= 2
    c = c + carry_ref[...]
    carry_ref[...] = c[:, tm - 1:tm]

    cum_rows = jnp.concatenate([c, jnp.zeros((LANES - c.shape[0], tm), F32)], axis=0).T
    for hd in range(B_HEADS):
        blk = kb[:, LANES * (hd // 2):LANES * (hd // 2 + 1)]
        if hd % 2:
            blk = pltpu.roll(blk, B_DIM, 1)
        neg = -LOG2_E * jnp.broadcast_to(cum_rows[:, hd:hd + 1], (tm, LANES))
        hi = neg.astype(BF16).astype(F32)
        rest = neg - hi
        mid = rest.astype(BF16).astype(F32)
        lo = rest - mid
        aug = jnp.where(lane < B_DIM, blk,
                        jnp.where(lane == B_DIM, hi,
                                  jnp.where(lane == B_DIM + 1, mid,
                                            jnp.where(lane == B_DIM + 2, lo, 0.0))))
        kbb_o[:, LANES * hd:LANES * (hd + 1)] = aug.astype(BF16)


def _projection(x2, scale, shift, norm_g, w_main, wf_t, bf_col, gains, rope_tabs, gm, transposed):
    t, d = x2.shape
    tm = PROJ_ROWS
    per_row = scale.shape[0] != 1
    mod_spec = (pl.BlockSpec((tm, d), lambda i: (i, 0)) if per_row
                else pl.BlockSpec((1, d), lambda i: (0, 0)))
    const = lambda shape: pl.BlockSpec(shape, lambda i: (0, 0))
    row = lambda width: pl.BlockSpec((tm, width), lambda i: (i, 0))
    col = lambda height: pl.BlockSpec((height, tm), lambda i: (0, i))
    slab = lambda height: pl.BlockSpec((1, height, tm), lambda i: (i, 0, 0))
    in_specs = [row(d), mod_spec, mod_spec, const((1, d)),
                const(w_main.shape), const(wf_t.shape), const(bf_col.shape),
                const((1, 512)), const((1, 512)), const((1, 512)), const((1, 512)),
                row(LANES), row(LANES), row(LANES), const(gm.shape)]
    f32o = jax.ShapeDtypeStruct((t, 512), F32)
    bf = lambda *shape: jax.ShapeDtypeStruct(shape, BF16)
    out_shape = [f32o] * 6 + [jax.ShapeDtypeStruct((B_HEADS, t), F32)]
    out_specs = [row(512)] * 6 + [col(B_HEADS)]
    if transposed:
        out_shape += [bf(512, t), bf(t, 512), bf(t // tm, A_HEADS * A_VT_ROWS, tm),
                      bf(B_HEADS * LANES, t), bf(t, B_HEADS * LANES),
                      bf(t // tm, B_HEADS * B_VT_ROWS, tm)]
        out_specs += [col(512), row(512), slab(A_HEADS * A_VT_ROWS),
                      col(B_HEADS * LANES), row(B_HEADS * LANES), slab(B_HEADS * B_VT_ROWS)]
    else:
        out_shape += [bf(t, 512)] * 5 + [bf(t, 1024)]
        out_specs += [row(512)] * 5 + [row(1024)]
    return pl.pallas_call(
        functools.partial(_proj_kernel, transposed=transposed),
        grid=(t // tm,),
        in_specs=in_specs,
        out_specs=out_specs,
        out_shape=out_shape,
        scratch_shapes=[pltpu.VMEM((16, 1), F32)],
        compiler_params=_params(1),
        name="projection_prompt" if transposed else "projection_decode",
    )(x2, scale, shift, norm_g, w_main, wf_t, bf_col, *gains, *rope_tabs, gm)


def _col_max(s):
    groups = s.reshape(8, s.shape[0] // 8, s.shape[1])
    return jnp.max(jnp.max(groups, axis=0), axis=0, keepdims=True)


def _kv_sweep(n_full, streams, last_mask, s_sc, mt_sc, m_sc, acc_sc):
    def produce(j, slot):
        for t, (q_cols, k_of, _) in enumerate(streams):
            s = jnp.dot(k_of(j), q_cols, preferred_element_type=F32)
            s_sc[t, slot] = s
            mt_sc[t, slot] = _col_max(s)

    def consume(j, slot, mask):
        for t, (_, _, v_of) in enumerate(streams):
            s = s_sc[t, slot]
            if mask is None:
                tile_max = mt_sc[t, slot]
            else:
                s = jnp.where(mask, s, -jnp.inf)
                tile_max = _col_max(s)
            m_prev = m_sc[t]
            m_new = jnp.maximum(m_prev, tile_max)
            p = jnp.exp2(s - m_new).astype(BF16)
            pv = None
            for c in range(KV_TILE // SLAB):
                part = jnp.dot(v_of(j * (KV_TILE // SLAB) + c), p[SLAB * c:SLAB * (c + 1)],
                               preferred_element_type=F32)
                pv = part if pv is None else pv + part
            acc_sc[t] = jnp.exp2(m_prev - m_new) * acc_sc[t] + pv
            m_sc[t] = m_new

    m_sc[...] = jnp.full(m_sc.shape, -jnp.inf, F32)
    acc_sc[...] = jnp.zeros_like(acc_sc)
    produce(0, 0)

    def body(i, carry):
        produce(2 * i + 1, 1)
        consume(2 * i, 0, None)
        produce(2 * i + 2, 0)
        consume(2 * i + 1, 1, None)
        return carry

    lax.fori_loop(0, n_full // 2, body, 0)

    @pl.when(n_full % 2 == 1)
    def _():
        produce(n_full, 1)
        consume(n_full - 1, 0, None)
        consume(n_full, 1, last_mask)

    @pl.when(n_full % 2 == 0)
    def _():
        consume(n_full, 0, last_mask)


def _sweep_scratch(rows):
    n = STEP_STREAMS
    return [pltpu.VMEM((n, 2, KV_TILE, Q_TILE), F32), pltpu.VMEM((n, 2, 1, Q_TILE), F32),
            pltpu.VMEM((n, 1, Q_TILE), F32), pltpu.VMEM((n, rows, Q_TILE), F32)]


def _resident(block_shape, index_map):
    return pl.BlockSpec(block_shape, index_map)


def _key_tile(k_ref, j, lanes):
    return k_ref[pl.ds(pl.multiple_of(j * KV_TILE, KV_TILE), KV_TILE), lanes]


def _lam(lq1_ref, lk1_ref, lq2_ref, lk2_ref, lam_init):
    return (jnp.exp(jnp.sum(lq1_ref[...] * lk1_ref[...], axis=1, keepdims=True))
            - jnp.exp(jnp.sum(lq2_ref[...] * lk2_ref[...], axis=1, keepdims=True)) + lam_init)


def _attn_a_kernel(qt_ref, k_ref, vt_ref, z_ref, lq1_ref, lk1_ref, lq2_ref, lk2_ref, sg_ref,
                   o_ref, s_sc, mt_sc, m_sc, acc_sc, *, lam_init):
    qi = pl.program_id(1)
    row = lax.broadcasted_iota(jnp.int32, (LANES, Q_TILE), 0)
    streams = []
    for hd in range(A_STEP_HEADS):
        lanes = slice(LANES * hd, LANES * (hd + 1))
        rows = slice(A_VT_ROWS * hd, A_VT_ROWS * (hd + 1))
        qt = qt_ref[lanes, :]
        zero = jnp.zeros_like(qt)
        k_of = functools.partial(_key_tile, k_ref, lanes=lanes)
        v_of = lambda sl, rows=rows: vt_ref[sl, rows, :]
        streams.append((jnp.where(row < A_QK_DIM, qt, zero), k_of, v_of))
        streams.append((jnp.where(row >= A_QK_DIM, qt, zero), k_of, v_of))
    shape = (KV_TILE, Q_TILE)
    chunk_mask = (lax.broadcasted_iota(jnp.int32, shape, 0) // CHUNK
                  <= lax.broadcasted_iota(jnp.int32, shape, 1) // CHUNK)
    _kv_sweep(qi, streams, chunk_mask, s_sc, mt_sc, m_sc, acc_sc)

    lam = _lam(lq1_ref, lk1_ref, lq2_ref, lk2_ref, lam_init)
    gain = sg_ref[...] * (1.0 - lam_init)
    for hd in range(A_STEP_HEADS):
        lanes = slice(LANES * hd, LANES * (hd + 1))
        a0, a1 = acc_sc[2 * hd], acc_sc[2 * hd + 1]
        oa = ((a0[:A_V_DIM] / a0[A_V_DIM:A_V_DIM + 1])
              - lam * (a1[:A_V_DIM] / a1[A_V_DIM:A_V_DIM + 1]))
        on = oa * lax.rsqrt(jnp.mean(oa * oa, axis=0, keepdims=True) + EPS) * gain
        o_ref[:, lanes] = (on.T * _silu(z_ref[:, lanes])).astype(o_ref.dtype)


def _attn_a_prompt(qa_t, kab, va_t, za, lam_vecs, subln_col, lam_init):
    t = kab.shape[0]
    nh = A_STEP_HEADS
    vec = pl.BlockSpec((1, A_QK_DIM), lambda g, i: (0, 0))
    return pl.pallas_call(
        functools.partial(_attn_a_kernel, lam_init=lam_init),
        grid=(A_HEADS // nh, t // Q_TILE),
        in_specs=[pl.BlockSpec((nh * LANES, Q_TILE), lambda g, i: (g, i)),
                  _resident((t, nh * LANES), lambda g, i: (0, g)),
                  _resident((t // SLAB, nh * A_VT_ROWS, SLAB), lambda g, i: (0, g, 0)),
                  pl.BlockSpec((Q_TILE, nh * LANES), lambda g, i: (i, g)),
                  vec, vec, vec, vec,
                  pl.BlockSpec((A_V_DIM, 1), lambda g, i: (0, 0))],
        out_specs=pl.BlockSpec((Q_TILE, nh * LANES), lambda g, i: (i, g)),
        out_shape=jax.ShapeDtypeStruct((t, A_WIDTH), BF16),
        scratch_shapes=_sweep_scratch(A_VT_ROWS),
        compiler_params=_params(2),
        name="attn_a_prompt",
    )(qa_t, kab, va_t, za, *lam_vecs, subln_col)


def _attn_b_kernel(qt_ref, k_ref, vt_ref, z_ref, o_ref, s_sc, mt_sc, m_sc, acc_sc):
    qi = pl.program_id(1)
    shape = (KV_TILE, Q_TILE)
    causal = lax.broadcasted_iota(jnp.int32, shape, 0) <= lax.broadcasted_iota(jnp.int32, shape, 1)
    streams = []
    for t in range(B_STEP_HEADS):
        lanes = slice(LANES * t, LANES * (t + 1))
        rows = slice(B_VT_ROWS * t, B_VT_ROWS * (t + 1))
        streams.append((qt_ref[lanes, :],
                        functools.partial(_key_tile, k_ref, lanes=lanes),
                        lambda sl, rows=rows: vt_ref[sl, rows, :]))
    _kv_sweep(qi, streams, causal, s_sc, mt_sc, m_sc, acc_sc)
    for p in range(B_STEP_HEADS // 2):
        lanes = slice(LANES * p, LANES * (p + 1))
        a0, a1 = acc_sc[2 * p], acc_sc[2 * p + 1]
        o = jnp.concatenate([a0[:B_DIM] / a0[B_DIM:B_DIM + 1], a1[:B_DIM] / a1[B_DIM:B_DIM + 1]],
                            axis=0)
        o_ref[:, lanes] = (o.T * _silu(z_ref[:, lanes])).astype(o_ref.dtype)


def _attn_b_prompt(qb_t, kb_aug, vb_t, zb):
    t = kb_aug.shape[0]
    nh = B_STEP_HEADS
    out_w = nh * B_DIM
    return pl.pallas_call(
        _attn_b_kernel,
        grid=(B_HEADS // nh, t // Q_TILE),
        in_specs=[pl.BlockSpec((nh * LANES, Q_TILE), lambda g, i: (g, i)),
                  _resident((t, nh * LANES), lambda g, i: (0, g)),
                  _resident((t // SLAB, nh * B_VT_ROWS, SLAB), lambda g, i: (0, g, 0)),
                  pl.BlockSpec((Q_TILE, out_w), lambda g, i: (i, g))],
        out_specs=pl.BlockSpec((Q_TILE, out_w), lambda g, i: (i, g)),
        out_shape=jax.ShapeDtypeStruct((t, B_WIDTH), BF16),
        scratch_shapes=_sweep_scratch(B_VT_ROWS),
        compiler_params=_params(2),
        name="attn_b_prompt",
    )(qb_t, kb_aug, vb_t, zb)


def _softmax_two_part(s_c, s_n, v_c, v_n, want_l):
    m = jnp.maximum(jnp.max(s_c, axis=1, keepdims=True), jnp.max(s_n, axis=1, keepdims=True))
    p_c = jnp.exp(s_c - m)
    p_n = jnp.exp(s_n - m)
    acc = (jnp.dot(p_c.astype(BF16), v_c, preferred_element_type=F32)
           + jnp.dot(p_n.astype(BF16), v_n, preferred_element_type=F32))
    l = None
    if want_l:
        l = jnp.sum(p_c, axis=1, keepdims=True) + jnp.sum(p_n, axis=1, keepdims=True)
    return acc, l


def _diff_epilogue(o0, o1, lam, subln_g, lam_init, z):
    oa = o0 - lam * o1
    on = oa * lax.rsqrt(jnp.mean(oa * oa, axis=-1, keepdims=True) + EPS)
    return on * subln_g * (1.0 - lam_init) * _silu(z)


def _fox_epilogue(acc0, acc1, z):
    lane = lax.broadcasted_iota(jnp.int32, acc0.shape, 1)
    o = jnp.where(lane < B_DIM, acc0 / acc0[:, B_DIM:B_DIM + 1], acc1 / acc1[:, 0:1])
    return o * _silu(z)


def _attn_dec_kernel(qa_ref, kan_ref, van_ref, za_ref, qb_ref, kbn_ref, vbn_ref, zb_ref,
                     cak_ref, cav_ref, cbk_ref, cbv_ref, lf_ref,
                     lq1_ref, lk1_ref, lq2_ref, lk2_ref, sg_ref,
                     ga_o, gb_o, *, lam_init):
    tdec = qa_ref.shape[0]
    past = cak_ref.shape[1]
    lane = lax.broadcasted_iota(jnp.int32, (tdec, LANES), 1)
    lam = _lam(lq1_ref, lk1_ref, lq2_ref, lk2_ref, lam_init)

    for h in range(A_HEADS):
        cols = slice(LANES * h, LANES * (h + 1))
        q = qa_ref[:, cols]
        zero = jnp.zeros_like(q)
        k_c = cak_ref[0, :, cols].astype(BF16)
        v_c = cav_ref[0, :, cols].astype(BF16)
        k_n = kan_ref[:, cols]
        v_n = van_ref[:, cols]
        outs = []
        for t in range(2):
            qm = jnp.where((lane >= A_QK_DIM) == (t == 1), q, zero)
            s_c = lax.dot_general(qm, k_c, _NT, preferred_element_type=F32)
            s_n = lax.dot_general(qm, k_n, _NT, preferred_element_type=F32)
            acc, l = _softmax_two_part(s_c, s_n, v_c, v_n, True)
            outs.append(acc / l)
        ga_o[:, cols] = _diff_epilogue(outs[0], outs[1], lam, sg_ref[...], lam_init,
                                       za_ref[:, cols]).astype(ga_o.dtype)

    lf = lf_ref[0]
    pos = lax.broadcasted_iota(jnp.int32, lf.shape, 1)
    cum = lf
    s = 1
    while s < lf.shape[1]:
        cum = cum + jnp.where(pos >= s, pltpu.roll(cum, s, 1), 0.0)
        s *= 2
    row = lax.broadcasted_iota(jnp.int32, (tdec, tdec), 0)
    col = lax.broadcasted_iota(jnp.int32, (tdec, tdec), 1)
    lane_c = lax.broadcasted_iota(jnp.int32, (past, LANES), 1)
    for p in range(B_HEADS // 2):
        cols = slice(LANES * p, LANES * (p + 1))
        q = qb_ref[:, cols]
        zero = jnp.zeros_like(q)
        k_c = cbk_ref[0, :, cols].astype(BF16)
        k_n = kbn_ref[:, cols]
        v_c = cbv_ref[0, :, cols]
        accs = []
        for t in range(2):
            qm = jnp.where((lane >= B_DIM) == (t == 1), q, zero)
            if t == 0:
                v_aug = jnp.where(lane_c < B_DIM, v_c, jnp.where(lane_c == B_DIM, 1.0, 0.0))
            else:
                v_aug = jnp.where(lane_c >= B_DIM, v_c, jnp.where(lane_c == 0, 1.0, 0.0))
            head = 2 * p + t
            s_c = lax.dot_general(qm, k_c, _NT, preferred_element_type=F32)
            s_c = s_c - cum[head:head + 1, :past]
            s_n = lax.dot_general(qm, k_n, _NT, preferred_element_type=F32)
            s_n = s_n - cum[head:head + 1, past:past + tdec]
            s_n = jnp.where(col <= row, s_n, -jnp.inf)
            v_n = vbn_ref[:, LANES * head:LANES * (head + 1)]
            acc, _ = _softmax_two_part(s_c, s_n, v_aug.astype(BF16), v_n, False)
            accs.append(acc)
        gb_o[:, cols] = _fox_epilogue(accs[0], accs[1], zb_ref[:, cols]).astype(gb_o.dtype)


def _attn_decode(proj, caches, lf_cat, lam_vecs, subln_g, lam_init, nb, tdec):
    qa, kab, vab, za, qb, kbb, vbaug, zb = proj
    cak, cav, cbk, cbv = caches
    past = cak.shape[1]
    rows = lambda w: pl.BlockSpec((tdec, w), lambda b: (b, 0))
    cache = pl.BlockSpec((1, past, 512), lambda b: (b, 0, 0))
    vec = pl.BlockSpec((1, A_QK_DIM), lambda b: (0, 0))
    return pl.pallas_call(
        functools.partial(_attn_dec_kernel, lam_init=lam_init),
        grid=(nb,),
        in_specs=[rows(512), rows(512), rows(512), rows(512),
                  rows(512), rows(512), rows(1024), rows(512),
                  cache, cache, cache, cache,
                  pl.BlockSpec((1, B_HEADS, lf_cat.shape[2]), lambda b: (b, 0, 0)),
                  vec, vec, vec, vec, pl.BlockSpec((1, A_V_DIM), lambda b: (0, 0))],
        out_specs=[rows(512), rows(512)],
        out_shape=[jax.ShapeDtypeStruct((nb * tdec, A_WIDTH), BF16),
                   jax.ShapeDtypeStruct((nb * tdec, B_WIDTH), BF16)],
        compiler_params=_params(1),
        name="attn_decode",
    )(qa, kab, vab, za, qb, kbb, vbaug, zb, cak, cav, cbk, cbv, lf_cat, *lam_vecs, subln_g)


def _out_kernel(x_ref, gate_ref, ga_ref, gb_ref, w_ref, y_ref):
    o = (jnp.dot(ga_ref[...], w_ref[:A_WIDTH, :], preferred_element_type=F32)
         + jnp.dot(gb_ref[...], w_ref[A_WIDTH:, :], preferred_element_type=F32))
    y_ref[...] = x_ref[...] + gate_ref[...] * o


def _out_projection(x2, gate, ga, gb, w_out):
    t, d = x2.shape
    tm = OUT_ROWS
    gate_spec = (pl.BlockSpec((tm, d), lambda i: (i, 0)) if gate.shape[0] != 1
                 else pl.BlockSpec((1, d), lambda i: (0, 0)))
    return pl.pallas_call(
        _out_kernel,
        grid=(t // tm,),
        in_specs=[pl.BlockSpec((tm, d), lambda i: (i, 0)), gate_spec,
                  pl.BlockSpec((tm, A_WIDTH), lambda i: (i, 0)),
                  pl.BlockSpec((tm, B_WIDTH), lambda i: (i, 0)),
                  pl.BlockSpec(w_out.shape, lambda i: (0, 0))],
        out_specs=pl.BlockSpec((tm, d), lambda i: (i, 0)),
        out_shape=jax.ShapeDtypeStruct((t, d), F32),
        compiler_params=_params(1),
        name="out_projection",
    )(x2, gate, ga, gb, w_out)


def _rope_tables(pos):
    half = ROT_DIM // 2
    inv = ROPE_THETA ** (-jnp.arange(0, ROT_DIM, 2, dtype=F32) / ROT_DIM)
    ang = pos.astype(F32)[:, None] * inv[None, :]
    cos, sin = jnp.cos(ang), jnp.sin(ang)
    n = pos.shape[0]
    pad = jnp.zeros((n, A_QK_DIM - ROT_DIM), F32)
    zeros = jnp.zeros((n, half), F32)
    c = jnp.concatenate([cos, cos, pad + 1.0], axis=1)
    s1 = jnp.concatenate([-sin, zeros, pad], axis=1)
    s2 = jnp.concatenate([zeros, sin, pad], axis=1)
    rep = LANES // A_QK_DIM
    return tuple(jnp.tile(a, (1, rep)) for a in (c, s1, s2))


def _group_mean_matrix():
    idx = np.arange(MXU_DIM) // A_QK_DIM
    return jnp.asarray((idx[:, None] == idx[None, :]).astype(np.float32) / A_QK_DIM, dtype=BF16)


def kernel(x_prompt, x_sample, cache_a_k, cache_a_v, cache_b_k, cache_b_v, cache_b_logf,
           c_prompt, c_sample, norm_g, w_ada, b_ada, w_in, b_f, qn_a, kn_a,
           lam_q1, lam_k1, lam_q2, lam_k2, subln_g, qn_b, kn_b, w_out):
    depth = norm_g.shape[0]
    bp, seq, d = x_prompt.shape
    bs, tdec, _ = x_sample.shape
    past = cache_a_k.shape[2]
    assert bp == 1 and seq % KV_TILE == 0 and Q_TILE % CHUNK == 0 and seq % OUT_ROWS == 0
    assert past % CHUNK == 0 and tdec == CHUNK and (bs * tdec) % PROJ_ROWS == 0

    xp = x_prompt.reshape(seq, d)
    xs = x_sample.reshape(bs * tdec, d)
    c_rows = bp + bs
    c_all = jnp.concatenate([c_prompt, c_sample, jnp.zeros((16 - c_rows, d), F32)], axis=0)
    rope_p = _rope_tables(jnp.arange(seq))
    rope_s = _rope_tables(jnp.tile(past + jnp.arange(tdec), bs))
    gm = _group_mean_matrix()
    fb_off = 2 * A_QK_WIDTH + 2 * A_WIDTH + 3 * B_WIDTH

    outs_p = [[] for _ in range(5)]
    outs_s = [[] for _ in range(5)]
    for l in range(depth):
        lam_init = 0.8 - 0.6 * math.exp(-0.3 * l)
        w_main = jnp.concatenate([w_in[l][:, :fb_off], w_in[l][:, fb_off + B_HEADS:]],
                                 axis=1).astype(BF16)
        wf_t = jnp.concatenate([w_in[l][:, fb_off:fb_off + B_HEADS].T,
                                jnp.zeros((16 - B_HEADS, d), F32)], axis=0).astype(BF16)
        bf_col = jnp.concatenate([b_f[l], jnp.zeros((16 - B_HEADS,), F32)]).reshape(16, 1)
        gains = tuple(jnp.tile(g[l], 512 // g.shape[1]).reshape(1, 512)
                      for g in (qn_a, kn_a, qn_b, kn_b))
        lam_vecs = tuple(v[l].reshape(1, A_QK_DIM) for v in (lam_q1, lam_k1, lam_q2, lam_k2))
        w_o = w_out[l].astype(BF16)
        g_row = norm_g[l].reshape(1, d)

        mod = _modulation(c_all, w_ada[l], b_ada[l])
        shift, scale, gate = mod[:, :d], mod[:, d:2 * d], mod[:, 2 * d:]

        (ka, va, za, kb, vb, zb, lf_t, qa_t, kab, va_t, qb_t, kb_aug, vb_t) = _projection(
            xp, scale[:bp], shift[:bp], g_row, w_main, wf_t, bf_col, gains, rope_p, gm, True)
        ga = _attn_a_prompt(qa_t, kab, va_t, za, lam_vecs, subln_g[l].reshape(A_V_DIM, 1), lam_init)
        gb = _attn_b_prompt(qb_t, kb_aug, vb_t, zb)
        xp = _out_projection(xp, gate[:bp], ga, gb, w_o)
        for dst, a in zip(outs_p, (ka.reshape(bp, seq, A_HEADS, 2, A_QK_DIM),
                                   va.reshape(bp, seq, A_HEADS, A_V_DIM),
                                   kb.reshape(bp, seq, B_HEADS, B_DIM),
                                   vb.reshape(bp, seq, B_HEADS, B_DIM),
                                   lf_t.T.reshape(bp, seq, B_HEADS))):
            dst.append(a)

        rep = lambda a: jnp.repeat(a[bp:c_rows], tdec, axis=0)
        (ka, va, za, kb, vb, zb, lf_t, qa, kab, vab, qb, kbb, vbaug) = _projection(
            xs, rep(scale), rep(shift), g_row, w_main, wf_t, bf_col, gains, rope_s, gm, False)
        lf_new = lf_t.reshape(B_HEADS, bs, tdec).transpose(1, 0, 2)
        lf_cat = jnp.concatenate([cache_b_logf[l].astype(F32).transpose(0, 2, 1), lf_new,
                                  jnp.zeros((bs, B_HEADS, LANES - tdec), F32)], axis=2)
        caches = (cache_a_k[l].reshape(bs, past, A_QK_WIDTH), cache_a_v[l].reshape(bs, past, A_WIDTH),
                  cache_b_k[l].reshape(bs, past, B_WIDTH), cache_b_v[l].reshape(bs, past, B_WIDTH))
        ga, gb = _attn_decode((qa, kab, vab, za, qb, kbb, vbaug, zb), caches, lf_cat,
                              lam_vecs, subln_g[l].reshape(1, A_V_DIM), lam_init, bs, tdec)
        xs = _out_projection(xs, rep(gate), ga, gb, w_o)
        for dst, a in zip(outs_s, (ka.reshape(bs, tdec, A_HEADS, 2, A_QK_DIM),
                                   va.reshape(bs, tdec, A_HEADS, A_V_DIM),
                                   kb.reshape(bs, tdec, B_HEADS, B_DIM),
                                   vb.reshape(bs, tdec, B_HEADS, B_DIM),
                                   lf_t.T.reshape(bs, tdec, B_HEADS))):
            dst.append(a)

    return (xp.reshape(bp, seq, d), xs.reshape(bs, tdec, d),
            *(jnp.stack(o) for o in outs_p), *(jnp.stack(o) for o in outs_s))
```

```python
import functools
import math

import numpy as np
import jax
import jax.numpy as jnp
from jax import lax
from jax.experimental import pallas as pl
from jax.experimental.pallas import tpu as pltpu

F32 = jnp.float32
BF16 = jnp.bfloat16

CHUNK = 64
A_HEADS = 4
A_QK_DIM = 64
A_V_DIM = 2 * A_QK_DIM
B_HEADS = 8
B_DIM = 64
A_WIDTH = A_HEADS * A_V_DIM
B_WIDTH = B_HEADS * B_DIM
A_QK_WIDTH = A_HEADS * 2 * A_QK_DIM
ROT_DIM = A_QK_DIM // 4
ROPE_THETA = 500000.0
EPS = 1e-6
LOG2_E = math.log2(math.e)

LANES = 128
BF16_ROWS = 16
MXU_DIM = 256
VMEM_LIMIT = 56 * 1024 * 1024

SLAB = MXU_DIM
PROJ_ROWS = SLAB
KV_TILE = 2 * SLAB
Q_TILE = KV_TILE
STEP_STREAMS = 2
A_STEP_HEADS = STEP_STREAMS // 2
B_STEP_HEADS = STEP_STREAMS
OUT_ROWS = 1024

A_VT_ROWS = A_V_DIM + BF16_ROWS
B_VT_ROWS = B_DIM + BF16_ROWS
B_BIAS_PIECES = 3
N_GROUPS = 8

_NT = (((1,), (1,)), ((), ()))


def _params(n_axes, flags=None):
    return pltpu.CompilerParams(dimension_semantics=("arbitrary",) * n_axes,
                                vmem_limit_bytes=VMEM_LIMIT, flags=flags)


def _silu(x):
    return x * jax.nn.sigmoid(x)


def _mod_kernel(c_ref, w_ref, b_ref, o_ref):
    a = _silu(c_ref[...]).astype(BF16)
    o_ref[...] = jnp.dot(a, w_ref[...].astype(BF16), preferred_element_type=F32) + b_ref[...]


def _modulation(c_all, w_ada, b_ada):
    rows, d = c_all.shape
    n = w_ada.shape[1]
    bn = 512
    return pl.pallas_call(
        _mod_kernel,
        grid=(n // bn,),
        in_specs=[pl.BlockSpec((rows, d), lambda j: (0, 0)),
                  pl.BlockSpec((d, bn), lambda j: (0, j)),
                  pl.BlockSpec((1, bn), lambda j: (0, j))],
        out_specs=pl.BlockSpec((rows, bn), lambda j: (0, j)),
        out_shape=jax.ShapeDtypeStruct((rows, n), F32),
        compiler_params=_params(1),
        name="modulation",
    )(c_all, w_ada, b_ada.reshape(1, n))


def _group_rms(u, gm):
    sq = u * u
    hi = sq.astype(BF16)
    lo = (sq - hi.astype(F32)).astype(BF16)
    parts = []
    for c in range(u.shape[1] // MXU_DIM):
        sl = slice(MXU_DIM * c, MXU_DIM * (c + 1))
        parts.append(jnp.dot(hi[:, sl], gm, preferred_element_type=F32)
                     + jnp.dot(lo[:, sl], gm, preferred_element_type=F32))
    ms = jnp.concatenate(parts, axis=1)
    return u * lax.rsqrt(ms + EPS)


def _rope(n, c, s1, s2):
    reps = n.shape[1] // LANES
    c = jnp.concatenate([c] * reps, axis=1)
    s1 = jnp.concatenate([s1] * reps, axis=1)
    s2 = jnp.concatenate([s2] * reps, axis=1)
    half = ROT_DIM // 2
    return (n * c + pltpu.roll(n, n.shape[1] - half, 1) * s1 + pltpu.roll(n, half, 1) * s2)


def _with_ones_rows(xt, rows, n_ones):
    r = lax.broadcasted_iota(jnp.int32, (rows, xt.shape[1]), 0)
    return jnp.concatenate([xt, jnp.where(r < n_ones, 1.0, 0.0)], axis=0)


def _proj_kernel(x_ref, scale_ref, shift_ref, g_ref, w_ref, wz_ref, wf_ref, bf_ref,
                 qna_ref, kna_ref, qnb_ref, knb_ref, rc_ref, rs1_ref, rs2_ref, gm_ref,
                 ka_o, va_o, za_o, kb_o, vb_o, zb_o, lf_o,
                 qa_o, kab_o, vab_o, qb_o, kbb_o, vbb_o, carry_ref, *, transposed):
    @pl.when(pl.program_id(0) == 0)
    def _():
        carry_ref[...] = jnp.zeros_like(carry_ref)

    tm = x_ref.shape[0]
    x = x_ref[...]
    y = x * lax.rsqrt(jnp.mean(x * x, axis=-1, keepdims=True) + EPS) * g_ref[...]
    h = y * (1.0 + scale_ref[...]) + shift_ref[...]
    hb = h.astype(BF16)
    gm = gm_ref[...]
    rc, rs1, rs2 = rc_ref[...], rs1_ref[...], rs2_ref[...]
    qk_scale = A_QK_DIM ** -0.5 * (LOG2_E if transposed else 1.0)
    lane = lax.broadcasted_iota(jnp.int32, (tm, LANES), 1)

    def group(i):
        w = wz_ref[...] if i == N_GROUPS - 1 else w_ref[:, 512 * i:512 * (i + 1)]
        return jnp.dot(hb, w, preferred_element_type=F32)

    fb = lax.dot_general(wf_ref[...], hb, _NT, preferred_element_type=F32)
    fx = fb + bf_ref[...]
    logf = jnp.minimum(fx, 0.0) - jnp.log1p(jnp.exp(-jnp.abs(fx)))
    lf_o[...] = logf[:B_HEADS]

    qa = _rope(_group_rms(group(0), gm) * qna_ref[...], rc, rs1, rs2) * qk_scale
    ka = _rope(_group_rms(group(1), gm) * kna_ref[...], rc, rs1, rs2)
    ka_o[...] = ka
    kab_o[...] = ka.astype(BF16)
    va = group(2)
    for hd in range(A_HEADS):
        va_o[pl.ds(hd, tm, stride=A_HEADS), :] = va[:, A_V_DIM * hd:A_V_DIM * (hd + 1)]
    za_o[...] = group(3)
    qb = _group_rms(group(4), gm) * qnb_ref[...] * qk_scale
    kb = _group_rms(group(5), gm) * knb_ref[...]
    kb_o[...] = kb
    vb = group(6)
    vb_o[...] = vb
    zb_o[...] = group(7)

    if not transposed:
        qa_o[...] = qa.astype(BF16)
        vab_o[...] = va.astype(BF16)
        qb_o[...] = qb.astype(BF16)
        kbb_o[...] = kb.astype(BF16)
        vbb_o[...] = vb.astype(BF16)
        return

    qa_o[...] = qa.T.astype(BF16)
    va_t = va.T
    for hd in range(A_HEADS):
        blk = _with_ones_rows(va_t[A_V_DIM * hd:A_V_DIM * (hd + 1)], BF16_ROWS, 1)
        vab_o[0, A_VT_ROWS * hd:A_VT_ROWS * (hd + 1), :] = blk.astype(BF16)
    qb_t = qb.T
    vb_t = vb.T
    for hd in range(B_HEADS):
        blk = _with_ones_rows(qb_t[B_DIM * hd:B_DIM * (hd + 1)], LANES - B_DIM, B_BIAS_PIECES)
        qb_o[LANES * hd:LANES * (hd + 1), :] = blk.astype(BF16)
        blk = _with_ones_rows(vb_t[B_DIM * hd:B_DIM * (hd + 1)], BF16_ROWS, 1)
        vbb_o[0, B_VT_ROWS * hd:B_VT_ROWS * (hd + 1), :] = blk.astype(BF16)

    pos = lax.broadcasted_iota(jnp.int32, logf.shape, 1)
    c = logf
    s = 1
    while s < tm:
        c = c + jnp.where(pos >= s, pltpu.roll(c, s, 1), 0.0)
        s *= 2
    c = c + carry_ref[...]
    carry_ref[...] = c[:, tm - 1:tm]

    cum_rows = jnp.concatenate([c, jnp.zeros((LANES - c.shape[0], tm), F32)], axis=0).T
    for hd in range(B_HEADS):
        blk = kb[:, LANES * (hd // 2):LANES * (hd // 2 + 1)]
        if hd % 2:
            blk = pltpu.roll(blk, B_DIM, 1)
        neg = -LOG2_E * jnp.broadcast_to(cum_rows[:, hd:hd + 1], (tm, LANES))
        hi = neg.astype(BF16).astype(F32)
        rest = neg - hi
        mid = rest.astype(BF16).astype(F32)
        lo = rest - mid
        aug = jnp.where(lane < B_DIM, blk,
                        jnp.where(lane == B_DIM, hi,
                                  jnp.where(lane == B_DIM + 1, mid,
                                            jnp.where(lane == B_DIM + 2, lo, 0.0))))
        kbb_o[:, LANES * hd:LANES * (hd + 1)] = aug.astype(BF16)


def _projection(x2, scale, shift, norm_g, w_all, w_zb, wf_t, bf_col, gains, rope_tabs, gm,
                transposed):
    t, d = x2.shape
    tm = PROJ_ROWS
    per_row = scale.shape[0] != 1
    mod_spec = (pl.BlockSpec((tm, d), lambda i: (i, 0)) if per_row
                else pl.BlockSpec((1, d), lambda i: (0, 0)))
    const = lambda shape: pl.BlockSpec(shape, lambda i: (0, 0))
    row = lambda width: pl.BlockSpec((tm, width), lambda i: (i, 0))
    col = lambda height: pl.BlockSpec((height, tm), lambda i: (0, i))
    slab = lambda height: pl.BlockSpec((1, height, tm), lambda i: (i, 0, 0))
    in_specs = [row(d), mod_spec, mod_spec, const((1, d)),
                const(w_all.shape), const(w_zb.shape), const(wf_t.shape), const(bf_col.shape),
                const((1, 512)), const((1, 512)), const((1, 512)), const((1, 512)),
                row(LANES), row(LANES), row(LANES), const(gm.shape)]
    f32o = jax.ShapeDtypeStruct((t, 512), F32)
    bf = lambda *shape: jax.ShapeDtypeStruct(shape, BF16)
    va_shape = jax.ShapeDtypeStruct((t * A_HEADS, A_V_DIM), F32)
    va_spec = pl.BlockSpec((tm * A_HEADS, A_V_DIM), lambda i: (i, 0))
    out_shape = [f32o, va_shape] + [f32o] * 4 + [jax.ShapeDtypeStruct((B_HEADS, t), F32)]
    out_specs = [row(512), va_spec] + [row(512)] * 4 + [col(B_HEADS)]
    if transposed:
        out_shape += [bf(512, t), bf(t, 512), bf(t // tm, A_HEADS * A_VT_ROWS, tm),
                      bf(B_HEADS * LANES, t), bf(t, B_HEADS * LANES),
                      bf(t // tm, B_HEADS * B_VT_ROWS, tm)]
        out_specs += [col(512), row(512), slab(A_HEADS * A_VT_ROWS),
                      col(B_HEADS * LANES), row(B_HEADS * LANES), slab(B_HEADS * B_VT_ROWS)]
    else:
        out_shape += [bf(t, 512)] * 6
        out_specs += [row(512)] * 6
    return pl.pallas_call(
        functools.partial(_proj_kernel, transposed=transposed),
        grid=(t // tm,),
        in_specs=in_specs,
        out_specs=out_specs,
        out_shape=out_shape,
        scratch_shapes=[pltpu.VMEM((16, 1), F32)],
        compiler_params=_params(1),
        name="projection_prompt" if transposed else "projection_decode",
    )(x2, scale, shift, norm_g, w_all, w_zb, wf_t, bf_col, *gains, *rope_tabs, gm)


def _col_max(s):
    groups = s.reshape(8, s.shape[0] // 8, s.shape[1])
    return jnp.max(jnp.max(groups, axis=0), axis=0, keepdims=True)


def _kv_sweep(n_full, streams, last_mask, s_sc, mt_sc, m_sc, acc_sc):
    def produce(j, slot):
        for t, (q_cols, k_of, _) in enumerate(streams):
            s = jnp.dot(k_of(j), q_cols, preferred_element_type=F32)
            s_sc[t, slot] = s
            mt_sc[t, slot] = _col_max(s)

    def consume(j, slot, mask):
        for t, (_, _, v_of) in enumerate(streams):
            s = s_sc[t, slot]
            if mask is None:
                tile_max = mt_sc[t, slot]
            else:
                s = jnp.where(mask, s, -jnp.inf)
                tile_max = _col_max(s)
            m_prev = m_sc[t]
            m_new = jnp.maximum(m_prev, tile_max)
            p = jnp.exp2(s - m_new).astype(BF16)
            pv = None
            for c in range(KV_TILE // SLAB):
                part = jnp.dot(v_of(j * (KV_TILE // SLAB) + c), p[SLAB * c:SLAB * (c + 1)],
                               preferred_element_type=F32)
                pv = part if pv is None else pv + part
            acc_sc[t] = jnp.exp2(m_prev - m_new) * acc_sc[t] + pv
            m_sc[t] = m_new

    m_sc[...] = jnp.full(m_sc.shape, -jnp.inf, F32)
    acc_sc[...] = jnp.zeros_like(acc_sc)
    produce(0, 0)

    def body(i, carry):
        produce(2 * i + 1, 1)
        consume(2 * i, 0, None)
        produce(2 * i + 2, 0)
        consume(2 * i + 1, 1, None)
        return carry

    lax.fori_loop(0, n_full // 2, body, 0)

    @pl.when(n_full % 2 == 1)
    def _():
        produce(n_full, 1)
        consume(n_full - 1, 0, None)
        consume(n_full, 1, last_mask)

    @pl.when(n_full % 2 == 0)
    def _():
        consume(n_full, 0, last_mask)


def _sweep_scratch(rows):
    n = STEP_STREAMS
    return [pltpu.VMEM((n, 2, KV_TILE, Q_TILE), F32), pltpu.VMEM((n, 2, 1, Q_TILE), F32),
            pltpu.VMEM((n, 1, Q_TILE), F32), pltpu.VMEM((n, rows, Q_TILE), F32)]


def _resident(block_shape, index_map):
    return pl.BlockSpec(block_shape, index_map)


def _key_tile(k_ref, j, lanes):
    return k_ref[pl.ds(pl.multiple_of(j * KV_TILE, KV_TILE), KV_TILE), lanes]


def _lam(lq1_ref, lk1_ref, lq2_ref, lk2_ref, lam_init):
    return (jnp.exp(jnp.sum(lq1_ref[...] * lk1_ref[...], axis=1, keepdims=True))
            - jnp.exp(jnp.sum(lq2_ref[...] * lk2_ref[...], axis=1, keepdims=True)) + lam_init)


def _attn_a_kernel(qt_ref, k_ref, vt_ref, z_ref, lq1_ref, lk1_ref, lq2_ref, lk2_ref, sg_ref,
                   o_ref, s_sc, mt_sc, m_sc, acc_sc, *, lam_init):
    qi = pl.program_id(1)
    row = lax.broadcasted_iota(jnp.int32, (LANES, Q_TILE), 0)
    streams = []
    for hd in range(A_STEP_HEADS):
        lanes = slice(LANES * hd, LANES * (hd + 1))
        rows = slice(A_VT_ROWS * hd, A_VT_ROWS * (hd + 1))
        qt = qt_ref[lanes, :]
        zero = jnp.zeros_like(qt)
        k_of = functools.partial(_key_tile, k_ref, lanes=lanes)
        v_of = lambda sl, rows=rows: vt_ref[sl, rows, :]
        streams.append((jnp.where(row < A_QK_DIM, qt, zero), k_of, v_of))
        streams.append((jnp.where(row >= A_QK_DIM, qt, zero), k_of, v_of))
    shape = (KV_TILE, Q_TILE)
    chunk_mask = (lax.broadcasted_iota(jnp.int32, shape, 0) // CHUNK
                  <= lax.broadcasted_iota(jnp.int32, shape, 1) // CHUNK)
    _kv_sweep(qi, streams, chunk_mask, s_sc, mt_sc, m_sc, acc_sc)

    lam = _lam(lq1_ref, lk1_ref, lq2_ref, lk2_ref, lam_init)
    gain = sg_ref[...] * (1.0 - lam_init)
    for hd in range(A_STEP_HEADS):
        lanes = slice(LANES * hd, LANES * (hd + 1))
        a0, a1 = acc_sc[2 * hd], acc_sc[2 * hd + 1]
        oa = ((a0[:A_V_DIM] / a0[A_V_DIM:A_V_DIM + 1])
              - lam * (a1[:A_V_DIM] / a1[A_V_DIM:A_V_DIM + 1]))
        on = oa * lax.rsqrt(jnp.mean(oa * oa, axis=0, keepdims=True) + EPS) * gain
        o_ref[:, lanes] = (on.T * _silu(z_ref[:, lanes])).astype(o_ref.dtype)


def _attn_a_prompt(qa_t, kab, va_t, za, lam_vecs, subln_col, lam_init):
    t = kab.shape[0]
    nh = A_STEP_HEADS
    vec = pl.BlockSpec((1, A_QK_DIM), lambda g, i: (0, 0))
    return pl.pallas_call(
        functools.partial(_attn_a_kernel, lam_init=lam_init),
        grid=(A_HEADS // nh, t // Q_TILE),
        in_specs=[pl.BlockSpec((nh * LANES, Q_TILE), lambda g, i: (g, i)),
                  _resident((t, nh * LANES), lambda g, i: (0, g)),
                  _resident((t // SLAB, nh * A_VT_ROWS, SLAB), lambda g, i: (0, g, 0)),
                  pl.BlockSpec((Q_TILE, nh * LANES), lambda g, i: (i, g)),
                  vec, vec, vec, vec,
                  pl.BlockSpec((A_V_DIM, 1), lambda g, i: (0, 0))],
        out_specs=pl.BlockSpec((Q_TILE, nh * LANES), lambda g, i: (i, g)),
        out_shape=jax.ShapeDtypeStruct((t, A_WIDTH), BF16),
        scratch_shapes=_sweep_scratch(A_VT_ROWS),
        compiler_params=_params(2),
        name="attn_a_prompt",
    )(qa_t, kab, va_t, za, *lam_vecs, subln_col)


def _attn_b_kernel(qt_ref, k_ref, vt_ref, z_ref, o_ref, s_sc, mt_sc, m_sc, acc_sc):
    qi = pl.program_id(1)
    shape = (KV_TILE, Q_TILE)
    causal = lax.broadcasted_iota(jnp.int32, shape, 0) <= lax.broadcasted_iota(jnp.int32, shape, 1)
    streams = []
    for t in range(B_STEP_HEADS):
        lanes = slice(LANES * t, LANES * (t + 1))
        rows = slice(B_VT_ROWS * t, B_VT_ROWS * (t + 1))
        streams.append((qt_ref[lanes, :],
                        functools.partial(_key_tile, k_ref, lanes=lanes),
                        lambda sl, rows=rows: vt_ref[sl, rows, :]))
    _kv_sweep(qi, streams, causal, s_sc, mt_sc, m_sc, acc_sc)
    for p in range(B_STEP_HEADS // 2):
        lanes = slice(LANES * p, LANES * (p + 1))
        a0, a1 = acc_sc[2 * p], acc_sc[2 * p + 1]
        o = jnp.concatenate([a0[:B_DIM] / a0[B_DIM:B_DIM + 1], a1[:B_DIM] / a1[B_DIM:B_DIM + 1]],
                            axis=0)
        o_ref[:, lanes] = (o.T * _silu(z_ref[:, lanes])).astype(o_ref.dtype)


def _attn_b_prompt(qb_t, kb_aug, vb_t, zb):
    t = kb_aug.shape[0]
    nh = B_STEP_HEADS
    out_w = nh * B_DIM
    return pl.pallas_call(
        _attn_b_kernel,
        grid=(B_HEADS // nh, t // Q_TILE),
        in_specs=[pl.BlockSpec((nh * LANES, Q_TILE), lambda g, i: (g, i)),
                  _resident((t, nh * LANES), lambda g, i: (0, g)),
                  _resident((t // SLAB, nh * B_VT_ROWS, SLAB), lambda g, i: (0, g, 0)),
                  pl.BlockSpec((Q_TILE, out_w), lambda g, i: (i, g))],
        out_specs=pl.BlockSpec((Q_TILE, out_w), lambda g, i: (i, g)),
        out_shape=jax.ShapeDtypeStruct((t, B_WIDTH), BF16),
        scratch_shapes=_sweep_scratch(B_VT_ROWS),
        compiler_params=_params(2),
        name="attn_b_prompt",
    )(qb_t, kb_aug, vb_t, zb)


def _softmax_two_part(s_c, s_n, pv_c, pv_n):
    m = jnp.maximum(jnp.max(s_c, axis=1, keepdims=True), jnp.max(s_n, axis=1, keepdims=True))
    p_c = jnp.exp(s_c - m)
    p_n = jnp.exp(s_n - m)
    acc = pv_c(p_c.astype(BF16)) + pv_n(p_n.astype(BF16))
    l = jnp.sum(p_c, axis=1, keepdims=True) + jnp.sum(p_n, axis=1, keepdims=True)
    return acc, l


def _diff_epilogue(o0, o1, lam, subln_g, lam_init, z):
    oa = o0 - lam * o1
    on = oa * lax.rsqrt(jnp.mean(oa * oa, axis=-1, keepdims=True) + EPS)
    return on * subln_g * (1.0 - lam_init) * _silu(z)


def _attn_dec_kernel(qa_ref, kan_ref, van_ref, za_ref, qb_ref, kbn_ref, vbn_ref, zb_ref,
                     cak_ref, cav_ref, cbk_ref, cbv_ref, lf_ref,
                     lq1_ref, lk1_ref, lq2_ref, lk2_ref, sg_ref,
                     ga_o, gb_o, *, lam_init):
    tdec = qa_ref.shape[0]
    past = cak_ref.shape[2]
    lane = lax.broadcasted_iota(jnp.int32, (tdec, LANES), 1)
    lam = _lam(lq1_ref, lk1_ref, lq2_ref, lk2_ref, lam_init)
    dot = functools.partial(jnp.dot, preferred_element_type=F32)
    dot_nt = lambda a, b: lax.dot_general(a, b, _NT, preferred_element_type=F32)

    for h in range(A_HEADS):
        cols = slice(LANES * h, LANES * (h + 1))
        q = qa_ref[:, cols]
        zero = jnp.zeros_like(q)
        kt_c = cak_ref[0, cols, :].astype(BF16)
        v_c = cav_ref[0, pl.ds(h, past, stride=A_HEADS), :].astype(BF16)
        k_n = kan_ref[:, cols]
        v_n = van_ref[:, cols]
        outs = []
        for t in range(2):
            qm = jnp.where((lane >= A_QK_DIM) == (t == 1), q, zero)
            acc, l = _softmax_two_part(dot(qm, kt_c), dot_nt(qm, k_n),
                                       lambda p: dot(p, v_c), lambda p: dot(p, v_n))
            outs.append(acc / l)
        ga_o[:, cols] = _diff_epilogue(outs[0], outs[1], lam, sg_ref[...], lam_init,
                                       za_ref[:, cols]).astype(ga_o.dtype)

    lf = lf_ref[0]
    pos = lax.broadcasted_iota(jnp.int32, lf.shape, 1)
    cum = lf
    s = 1
    while s < lf.shape[1]:
        cum = cum + jnp.where(pos >= s, pltpu.roll(cum, s, 1), 0.0)
        s *= 2
    row = lax.broadcasted_iota(jnp.int32, (tdec, tdec), 0)
    col = lax.broadcasted_iota(jnp.int32, (tdec, tdec), 1)
    for p in range(B_HEADS // 2):
        cols = slice(LANES * p, LANES * (p + 1))
        q = qb_ref[:, cols]
        zero = jnp.zeros_like(q)
        kt_c = cbk_ref[0, cols, :].astype(BF16)
        vt_c = cbv_ref[0, cols, :].astype(BF16)
        k_n = kbn_ref[:, cols]
        v_n = vbn_ref[:, cols]
        outs = []
        for t in range(2):
            qm = jnp.where((lane >= B_DIM) == (t == 1), q, zero)
            head = 2 * p + t
            s_c = dot(qm, kt_c) - cum[head:head + 1, :past]
            s_n = dot_nt(qm, k_n) - cum[head:head + 1, past:past + tdec]
            s_n = jnp.where(col <= row, s_n, -jnp.inf)
            acc, l = _softmax_two_part(s_c, s_n, lambda pr: dot_nt(pr, vt_c),
                                       lambda pr: dot(pr, v_n))
            outs.append(acc / l)
        o = jnp.where(lane < B_DIM, outs[0], outs[1])
        gb_o[:, cols] = (o * _silu(zb_ref[:, cols])).astype(gb_o.dtype)


def _attn_decode(proj, caches, lf_cat, lam_vecs, subln_g, lam_init, nb, tdec):
    cak_t, cav, cbk_t, cbv_t = caches
    past = cak_t.shape[2]
    rows = pl.BlockSpec((tdec, 512), lambda b: (b, 0))
    feat_major = pl.BlockSpec((1, 512, past), lambda b: (b, 0, 0))
    vec = pl.BlockSpec((1, A_QK_DIM), lambda b: (0, 0))
    return pl.pallas_call(
        functools.partial(_attn_dec_kernel, lam_init=lam_init),
        grid=(nb,),
        in_specs=[rows] * 8
                 + [feat_major, pl.BlockSpec((1, past * A_HEADS, A_V_DIM), lambda b: (b, 0, 0)),
                    feat_major, feat_major,
                    pl.BlockSpec((1, B_HEADS, lf_cat.shape[2]), lambda b: (b, 0, 0)),
                    vec, vec, vec, vec, pl.BlockSpec((1, A_V_DIM), lambda b: (0, 0))],
        out_specs=[rows, rows],
        out_shape=[jax.ShapeDtypeStruct((nb * tdec, A_WIDTH), BF16),
                   jax.ShapeDtypeStruct((nb * tdec, B_WIDTH), BF16)],
        compiler_params=_params(1),
        name="attn_decode",
    )(*proj, cak_t, cav, cbk_t, cbv_t, lf_cat, *lam_vecs, subln_g)


def _out_kernel(x_ref, gate_ref, ga_ref, gb_ref, w_ref, y_ref):
    o = (jnp.dot(ga_ref[...], w_ref[:A_WIDTH, :], preferred_element_type=F32)
         + jnp.dot(gb_ref[...], w_ref[A_WIDTH:, :], preferred_element_type=F32))
    y_ref[...] = x_ref[...] + gate_ref[...] * o


def _out_projection(x2, gate, ga, gb, w_out):
    t, d = x2.shape
    tm = min(OUT_ROWS, t)
    gate_spec = (pl.BlockSpec((tm, d), lambda i: (i, 0)) if gate.shape[0] != 1
                 else pl.BlockSpec((1, d), lambda i: (0, 0)))
    return pl.pallas_call(
        _out_kernel,
        grid=(t // tm,),
        in_specs=[pl.BlockSpec((tm, d), lambda i: (i, 0)), gate_spec,
                  pl.BlockSpec((tm, A_WIDTH), lambda i: (i, 0)),
                  pl.BlockSpec((tm, B_WIDTH), lambda i: (i, 0)),
                  pl.BlockSpec(w_out.shape, lambda i: (0, 0))],
        out_specs=pl.BlockSpec((tm, d), lambda i: (i, 0)),
        out_shape=jax.ShapeDtypeStruct((t, d), F32),
        compiler_params=_params(1),
        name="out_projection",
    )(x2, gate, ga, gb, w_out)


def _rope_tables(pos):
    half = ROT_DIM // 2
    inv = ROPE_THETA ** (-jnp.arange(0, ROT_DIM, 2, dtype=F32) / ROT_DIM)
    ang = pos.astype(F32)[:, None] * inv[None, :]
    cos, sin = jnp.cos(ang), jnp.sin(ang)
    n = pos.shape[0]
    pad = jnp.zeros((n, A_QK_DIM - ROT_DIM), F32)
    zeros = jnp.zeros((n, half), F32)
    rep = LANES // A_QK_DIM
    c = jnp.concatenate([cos, cos, pad + 1.0] * rep, axis=1)
    s1 = jnp.concatenate([-sin, zeros, pad] * rep, axis=1)
    s2 = jnp.concatenate([zeros, sin, pad] * rep, axis=1)
    return c, s1, s2


def _group_mean_matrix():
    idx = np.arange(MXU_DIM) // A_QK_DIM
    return jnp.asarray((idx[:, None] == idx[None, :]).astype(np.float32) / A_QK_DIM, dtype=BF16)


def kernel(x_prompt, x_sample, cache_a_k, cache_a_v, cache_b_k, cache_b_v, cache_b_logf,
           c_prompt, c_sample, norm_g, w_ada, b_ada, w_in, b_f, qn_a, kn_a,
           lam_q1, lam_k1, lam_q2, lam_k2, subln_g, qn_b, kn_b, w_out):
    depth = norm_g.shape[0]
    bp, seq, d = x_prompt.shape
    bs, tdec, _ = x_sample.shape
    past = cache_a_k.shape[2]
    assert bp == 1 and seq % KV_TILE == 0 and Q_TILE % CHUNK == 0 and seq % OUT_ROWS == 0
    assert past % CHUNK == 0 and tdec == CHUNK and (bs * tdec) % PROJ_ROWS == 0

    xp = x_prompt.reshape(seq, d)
    xs = x_sample.reshape(bs * tdec, d)
    c_rows = bp + bs
    c_all = jnp.concatenate([c_prompt, c_sample, jnp.zeros((16 - c_rows, d), F32)], axis=0)
    rope_p = _rope_tables(jnp.arange(seq))
    rope_s = _rope_tables(jnp.tile(past + jnp.arange(tdec), bs))
    gm = _group_mean_matrix()
    fb_off = 2 * A_QK_WIDTH + 2 * A_WIDTH + 3 * B_WIDTH

    outs_p = [[] for _ in range(5)]
    outs_s = [[] for _ in range(5)]
    for l in range(depth):
        lam_init = 0.8 - 0.6 * math.exp(-0.3 * l)
        w_all = w_in[l].astype(BF16)
        w_zb = w_all[:, fb_off + B_HEADS:]
        wf_t = jnp.concatenate([w_in[l][:, fb_off:fb_off + B_HEADS].T.astype(BF16),
                                jnp.zeros((16 - B_HEADS, d), BF16)], axis=0)
        bf_col = jnp.concatenate([b_f[l], jnp.zeros((16 - B_HEADS,), F32)]).reshape(16, 1)
        gains = tuple(jnp.tile(g[l], 512 // g.shape[1]).reshape(1, 512)
                      for g in (qn_a, kn_a, qn_b, kn_b))
        lam_vecs = tuple(v[l].reshape(1, A_QK_DIM) for v in (lam_q1, lam_k1, lam_q2, lam_k2))
        w_o = w_out[l].astype(BF16)
        g_row = norm_g[l].reshape(1, d)

        mod = _modulation(c_all, w_ada[l], b_ada[l])
        shift, scale, gate = mod[:, :d], mod[:, d:2 * d], mod[:, 2 * d:]

        (ka, va, za, kb, vb, zb, lf_t, qa_t, kab, va_t, qb_t, kb_aug, vb_t) = _projection(
            xp, scale[:bp], shift[:bp], g_row, w_all, w_zb, wf_t, bf_col, gains, rope_p, gm, True)
        ga = _attn_a_prompt(qa_t, kab, va_t, za, lam_vecs, subln_g[l].reshape(A_V_DIM, 1), lam_init)
        gb = _attn_b_prompt(qb_t, kb_aug, vb_t, zb)
        xp = _out_projection(xp, gate[:bp], ga, gb, w_o)
        for dst, a in zip(outs_p, (ka.reshape(bp, seq, A_HEADS, 2, A_QK_DIM),
                                   va.reshape(bp, seq, A_HEADS, A_V_DIM),
                                   kb.reshape(bp, seq, B_HEADS, B_DIM),
                                   vb.reshape(bp, seq, B_HEADS, B_DIM),
                                   lf_t.T.reshape(bp, seq, B_HEADS))):
            dst.append(a)

        rep = lambda a: jnp.repeat(a[bp:c_rows], tdec, axis=0)
        (ka, va, za, kb, vb, zb, lf_t, qa, kab, vab, qb, kbb, vbb) = _projection(
            xs, rep(scale), rep(shift), g_row, w_all, w_zb, wf_t, bf_col, gains, rope_s, gm,
            False)
        lf_new = lf_t.reshape(B_HEADS, bs, tdec).transpose(1, 0, 2)
        lf_cat = jnp.concatenate([cache_b_logf[l].astype(F32).transpose(0, 2, 1), lf_new,
                                  jnp.zeros((bs, B_HEADS, LANES - tdec), F32)], axis=2)
        caches = (cache_a_k[l].transpose(0, 2, 3, 4, 1).reshape(bs, A_QK_WIDTH, past),
                  cache_a_v[l].reshape(bs, past * A_HEADS, A_V_DIM),
                  cache_b_k[l].transpose(0, 2, 3, 1).reshape(bs, B_WIDTH, past),
                  cache_b_v[l].transpose(0, 2, 3, 1).reshape(bs, B_WIDTH, past))
        ga, gb = _attn_decode((qa, kab, vab, za, qb, kbb, vbb, zb), caches, lf_cat,
                              lam_vecs, subln_g[l].reshape(1, A_V_DIM), lam_init, bs, tdec)
        xs = _out_projection(xs, rep(gate), ga, gb, w_o)
        for dst, a in zip(outs_s, (ka.reshape(bs, tdec, A_HEADS, 2, A_QK_DIM),
                                   va.reshape(bs, tdec, A_HEADS, A_V_DIM),
                                   kb.reshape(bs, tdec, B_HEADS, B_DIM),
                                   vb.reshape(bs, tdec, B_HEADS, B_DIM),
                                   lf_t.T.reshape(bs, tdec, B_HEADS))):
            dst.append(a)

    return (xp.reshape(bp, seq, d), xs.reshape(bs, tdec, d),
            *(jnp.stack(o) for o in outs_p), *(jnp.stack(o) for o in outs_s))
```

```python
import functools
import math

import numpy as np
import jax
import jax.numpy as jnp
from jax import lax
from jax.experimental import pallas as pl
from jax.experimental.pallas import tpu as pltpu

F32 = jnp.float32
BF16 = jnp.bfloat16

CHUNK = 64
A_HEADS = 4
A_QK_DIM = 64
A_V_DIM = 2 * A_QK_DIM
B_HEADS = 8
B_DIM = 64
A_WIDTH = A_HEADS * A_V_DIM
B_WIDTH = B_HEADS * B_DIM
A_QK_WIDTH = A_HEADS * 2 * A_QK_DIM
ROT_DIM = A_QK_DIM // 4
ROPE_THETA = 500000.0
EPS = 1e-6
LOG2_E = math.log2(math.e)

LANES = 128
BF16_ROWS = 16
MXU_DIM = 256
VMEM_LIMIT = 56 * 1024 * 1024

SLAB = MXU_DIM
PROJ_ROWS = SLAB
KV_TILE = 2 * SLAB
Q_TILE = KV_TILE
STEP_STREAMS = 2
A_STEP_HEADS = STEP_STREAMS // 2
B_STEP_HEADS = STEP_STREAMS
OUT_ROWS = 1024

A_VT_ROWS = A_V_DIM + BF16_ROWS
B_VT_ROWS = B_DIM + BF16_ROWS
B_BIAS_PIECES = 3
N_GROUPS = 8

_NT = (((1,), (1,)), ((), ()))


def _params(n_axes):
    return pltpu.CompilerParams(dimension_semantics=("arbitrary",) * n_axes,
                                vmem_limit_bytes=VMEM_LIMIT)


def _silu(x):
    return x * jax.nn.sigmoid(x)


def _mod_kernel(c_ref, w_ref, b_ref, o_ref):
    a = _silu(c_ref[...]).astype(BF16)
    o_ref[...] = jnp.dot(a, w_ref[...].astype(BF16), preferred_element_type=F32) + b_ref[...]


def _modulation(c_all, w_ada, b_ada):
    rows, d = c_all.shape
    n = w_ada.shape[1]
    bn = 512
    return pl.pallas_call(
        _mod_kernel,
        grid=(n // bn,),
        in_specs=[pl.BlockSpec((rows, d), lambda j: (0, 0)),
                  pl.BlockSpec((d, bn), lambda j: (0, j)),
                  pl.BlockSpec((1, bn), lambda j: (0, j))],
        out_specs=pl.BlockSpec((rows, bn), lambda j: (0, j)),
        out_shape=jax.ShapeDtypeStruct((rows, n), F32),
        compiler_params=_params(1),
        name="modulation",
    )(c_all, w_ada, b_ada.reshape(1, n))


def _group_rms(u, gm):
    sq = u * u
    hi = sq.astype(BF16)
    lo = (sq - hi.astype(F32)).astype(BF16)
    parts = []
    for c in range(u.shape[1] // MXU_DIM):
        sl = slice(MXU_DIM * c, MXU_DIM * (c + 1))
        parts.append(jnp.dot(hi[:, sl], gm, preferred_element_type=F32)
                     + jnp.dot(lo[:, sl], gm, preferred_element_type=F32))
    ms = jnp.concatenate(parts, axis=1)
    return u * lax.rsqrt(ms + EPS)


def _rope_t(xt, cos_t, sin_t):
    half = ROT_DIM // 2
    pieces = []
    for base in range(0, xt.shape[0], A_QK_DIM):
        x1 = xt[base:base + half]
        x2 = xt[base + half:base + ROT_DIM]
        pieces += [x1 * cos_t - x2 * sin_t, x2 * cos_t + x1 * sin_t,
                   xt[base + ROT_DIM:base + A_QK_DIM]]
    return jnp.concatenate(pieces, axis=0)


def _with_ones_rows(xt, rows, n_ones):
    r = lax.broadcasted_iota(jnp.int32, (rows, xt.shape[1]), 0)
    return jnp.concatenate([xt, jnp.where(r < n_ones, 1.0, 0.0)], axis=0)


def _proj_kernel(x_ref, scale_ref, shift_ref, g_ref, w_ref, wz_ref, wf_ref, bf_ref,
                 qna_ref, kna_ref, qnb_ref, knb_ref, cos_ref, sin_ref, gm_ref,
                 ka_o, va_o, za_o, kb_o, vb_o, zb_o, lf_o,
                 qa_o, kab_o, vab_o, qb_o, kbb_o, vbb_o, carry_ref, *, transposed):
    @pl.when(pl.program_id(0) == 0)
    def _():
        carry_ref[...] = jnp.zeros_like(carry_ref)

    tm = x_ref.shape[0]
    x = x_ref[...]
    y = x * lax.rsqrt(jnp.mean(x * x, axis=-1, keepdims=True) + EPS) * g_ref[...]
    h = y * (1.0 + scale_ref[...]) + shift_ref[...]
    hb = h.astype(BF16)
    gm = gm_ref[...]
    cos_t, sin_t = cos_ref[...], sin_ref[...]
    qk_scale = A_QK_DIM ** -0.5 * (LOG2_E if transposed else 1.0)
    lane = lax.broadcasted_iota(jnp.int32, (tm, LANES), 1)

    def group(i):
        w = wz_ref[...] if i == N_GROUPS - 1 else w_ref[:, 512 * i:512 * (i + 1)]
        return jnp.dot(hb, w, preferred_element_type=F32)

    fb = lax.dot_general(wf_ref[...], hb, _NT, preferred_element_type=F32)
    fx = fb + bf_ref[...]
    logf = jnp.minimum(fx, 0.0) - jnp.log1p(jnp.exp(-jnp.abs(fx)))
    lf_o[...] = logf[:B_HEADS]

    qa_t = _rope_t((_group_rms(group(0), gm) * qna_ref[...]).T, cos_t, sin_t) * qk_scale
    ka_t = _rope_t((_group_rms(group(1), gm) * kna_ref[...]).T, cos_t, sin_t)
    ka = ka_t.T
    ka_o[...] = ka_t if transposed else ka
    kab_o[...] = ka.astype(BF16)
    va = group(2)
    for hd in range(A_HEADS):
        va_o[pl.ds(hd, tm, stride=A_HEADS), :] = va[:, A_V_DIM * hd:A_V_DIM * (hd + 1)]
    za_o[...] = group(3)
    qb = _group_rms(group(4), gm) * qnb_ref[...] * qk_scale
    kb = _group_rms(group(5), gm) * knb_ref[...]
    kb_o[...] = kb
    vb = group(6)
    vb_o[...] = vb
    zb_o[...] = group(7)

    if not transposed:
        qa_o[...] = qa_t.T.astype(BF16)
        vab_o[...] = va.astype(BF16)
        qb_o[...] = qb.astype(BF16)
        kbb_o[...] = kb.astype(BF16)
        vbb_o[...] = vb.astype(BF16)
        return

    qa_o[...] = qa_t.astype(BF16)
    va_t = va.T
    for hd in range(A_HEADS):
        blk = _with_ones_rows(va_t[A_V_DIM * hd:A_V_DIM * (hd + 1)], BF16_ROWS, 1)
        vab_o[0, A_VT_ROWS * hd:A_VT_ROWS * (hd + 1), :] = blk.astype(BF16)
    qb_t = qb.T
    vb_t = vb.T
    for hd in range(B_HEADS):
        blk = _with_ones_rows(qb_t[B_DIM * hd:B_DIM * (hd + 1)], LANES - B_DIM, B_BIAS_PIECES)
        qb_o[LANES * hd:LANES * (hd + 1), :] = blk.astype(BF16)
        blk = _with_ones_rows(vb_t[B_DIM * hd:B_DIM * (hd + 1)], BF16_ROWS, 1)
        vbb_o[0, B_VT_ROWS * hd:B_VT_ROWS * (hd + 1), :] = blk.astype(BF16)

    pos = lax.broadcasted_iota(jnp.int32, logf.shape, 1)
    c = logf
    s = 1
    while s < tm:
        c = c + jnp.where(pos >= s, pltpu.roll(c, s, 1), 0.0)
        s *= 2
    c = c + carry_ref[...]
    carry_ref[...] = c[:, tm - 1:tm]

    cum_rows = jnp.concatenate([c, jnp.zeros((LANES - c.shape[0], tm), F32)], axis=0).T
    for hd in range(B_HEADS):
        blk = kb[:, LANES * (hd // 2):LANES * (hd // 2 + 1)]
        if hd % 2:
            blk = pltpu.roll(blk, B_DIM, 1)
        neg = -LOG2_E * jnp.broadcast_to(cum_rows[:, hd:hd + 1], (tm, LANES))
        hi = neg.astype(BF16).astype(F32)
        rest = neg - hi
        mid = rest.astype(BF16).astype(F32)
        lo = rest - mid
        aug = jnp.where(lane < B_DIM, blk,
                        jnp.where(lane == B_DIM, hi,
                                  jnp.where(lane == B_DIM + 1, mid,
                                            jnp.where(lane == B_DIM + 2, lo, 0.0))))
        kbb_o[:, LANES * hd:LANES * (hd + 1)] = aug.astype(BF16)


def _projection(x2, scale, shift, norm_g, w_all, w_zb, wf_t, bf_col, gains, rope_tabs, gm,
                transposed):
    t, d = x2.shape
    tm = PROJ_ROWS
    per_row = scale.shape[0] != 1
    mod_spec = (pl.BlockSpec((tm, d), lambda i: (i, 0)) if per_row
                else pl.BlockSpec((1, d), lambda i: (0, 0)))
    const = lambda shape: pl.BlockSpec(shape, lambda i: (0, 0))
    row = lambda width: pl.BlockSpec((tm, width), lambda i: (i, 0))
    col = lambda height: pl.BlockSpec((height, tm), lambda i: (0, i))
    slab = lambda height: pl.BlockSpec((1, height, tm), lambda i: (i, 0, 0))
    in_specs = [row(d), mod_spec, mod_spec, const((1, d)),
                const(w_all.shape), const(w_zb.shape), const(wf_t.shape), const(bf_col.shape),
                const((1, 512)), const((1, 512)), const((1, 512)), const((1, 512)),
                col(ROT_DIM // 2), col(ROT_DIM // 2), const(gm.shape)]
    f32o = jax.ShapeDtypeStruct((t, 512), F32)
    bf = lambda *shape: jax.ShapeDtypeStruct(shape, BF16)
    va_shape = jax.ShapeDtypeStruct((t * A_HEADS, A_V_DIM), F32)
    va_spec = pl.BlockSpec((tm * A_HEADS, A_V_DIM), lambda i: (i, 0))
    ka_shape = jax.ShapeDtypeStruct((512, t), F32) if transposed else f32o
    ka_spec = col(512) if transposed else row(512)
    out_shape = [ka_shape, va_shape] + [f32o] * 4 + [jax.ShapeDtypeStruct((B_HEADS, t), F32)]
    out_specs = [ka_spec, va_spec] + [row(512)] * 4 + [col(B_HEADS)]
    if transposed:
        out_shape += [bf(512, t), bf(t, 512), bf(t // tm, A_HEADS * A_VT_ROWS, tm),
                      bf(B_HEADS * LANES, t), bf(t, B_HEADS * LANES),
                      bf(t // tm, B_HEADS * B_VT_ROWS, tm)]
        out_specs += [col(512), row(512), slab(A_HEADS * A_VT_ROWS),
                      col(B_HEADS * LANES), row(B_HEADS * LANES), slab(B_HEADS * B_VT_ROWS)]
    else:
        out_shape += [bf(t, 512)] * 6
        out_specs += [row(512)] * 6
    return pl.pallas_call(
        functools.partial(_proj_kernel, transposed=transposed),
        grid=(t // tm,),
        in_specs=in_specs,
        out_specs=out_specs,
        out_shape=out_shape,
        scratch_shapes=[pltpu.VMEM((16, 1), F32)],
        compiler_params=_params(1),
        name="projection_prompt" if transposed else "projection_decode",
    )(x2, scale, shift, norm_g, w_all, w_zb, wf_t, bf_col, *gains, *rope_tabs, gm)


def _col_max(s):
    groups = s.reshape(8, s.shape[0] // 8, s.shape[1])
    return jnp.max(jnp.max(groups, axis=0), axis=0, keepdims=True)


def _kv_sweep(n_full, streams, last_mask, s_sc, mt_sc, m_sc, acc_sc):
    def produce(j, slot):
        for t, (q_cols, k_of, _) in enumerate(streams):
            s = jnp.dot(k_of(j), q_cols, preferred_element_type=F32)
            s_sc[t, slot] = s
            mt_sc[t, slot] = _col_max(s)

    def consume(j, slot, mask):
        for t, (_, _, v_of) in enumerate(streams):
            s = s_sc[t, slot]
            if mask is None:
                tile_max = mt_sc[t, slot]
            else:
                s = jnp.where(mask, s, -jnp.inf)
                tile_max = _col_max(s)
            m_prev = m_sc[t]
            m_new = jnp.maximum(m_prev, tile_max)
            p = jnp.exp2(s - m_new).astype(BF16)
            pv = None
            for c in range(KV_TILE // SLAB):
                part = jnp.dot(v_of(j * (KV_TILE // SLAB) + c), p[SLAB * c:SLAB * (c + 1)],
                               preferred_element_type=F32)
                pv = part if pv is None else pv + part
            acc_sc[t] = jnp.exp2(m_prev - m_new) * acc_sc[t] + pv
            m_sc[t] = m_new

    m_sc[...] = jnp.full(m_sc.shape, -jnp.inf, F32)
    acc_sc[...] = jnp.zeros_like(acc_sc)
    produce(0, 0)

    def body(i, carry):
        produce(2 * i + 1, 1)
        consume(2 * i, 0, None)
        produce(2 * i + 2, 0)
        consume(2 * i + 1, 1, None)
        return carry

    lax.fori_loop(0, n_full // 2, body, 0)

    @pl.when(n_full % 2 == 1)
    def _():
        produce(n_full, 1)
        consume(n_full - 1, 0, None)
        consume(n_full, 1, last_mask)

    @pl.when(n_full % 2 == 0)
    def _():
        consume(n_full, 0, last_mask)


def _sweep_scratch(rows):
    n = STEP_STREAMS
    return [pltpu.VMEM((n, 2, KV_TILE, Q_TILE), F32), pltpu.VMEM((n, 2, 1, Q_TILE), F32),
            pltpu.VMEM((n, 1, Q_TILE), F32), pltpu.VMEM((n, rows, Q_TILE), F32)]


def _resident(block_shape, index_map):
    return pl.BlockSpec(block_shape, index_map)


def _key_tile(k_ref, j, lanes):
    return k_ref[pl.ds(pl.multiple_of(j * KV_TILE, KV_TILE), KV_TILE), lanes]


def _lam(lq1_ref, lk1_ref, lq2_ref, lk2_ref, lam_init):
    return (jnp.exp(jnp.sum(lq1_ref[...] * lk1_ref[...], axis=1, keepdims=True))
            - jnp.exp(jnp.sum(lq2_ref[...] * lk2_ref[...], axis=1, keepdims=True)) + lam_init)


def _attn_a_kernel(qt_ref, k_ref, vt_ref, z_ref, lq1_ref, lk1_ref, lq2_ref, lk2_ref, sg_ref,
                   o_ref, s_sc, mt_sc, m_sc, acc_sc, *, lam_init):
    qi = pl.program_id(1)
    row = lax.broadcasted_iota(jnp.int32, (LANES, Q_TILE), 0)
    streams = []
    for hd in range(A_STEP_HEADS):
        lanes = slice(LANES * hd, LANES * (hd + 1))
        rows = slice(A_VT_ROWS * hd, A_VT_ROWS * (hd + 1))
        qt = qt_ref[lanes, :]
        zero = jnp.zeros_like(qt)
        k_of = functools.partial(_key_tile, k_ref, lanes=lanes)
        v_of = lambda sl, rows=rows: vt_ref[sl, rows, :]
        streams.append((jnp.where(row < A_QK_DIM, qt, zero), k_of, v_of))
        streams.append((jnp.where(row >= A_QK_DIM, qt, zero), k_of, v_of))
    shape = (KV_TILE, Q_TILE)
    chunk_mask = (lax.broadcasted_iota(jnp.int32, shape, 0) // CHUNK
                  <= lax.broadcasted_iota(jnp.int32, shape, 1) // CHUNK)
    _kv_sweep(qi, streams, chunk_mask, s_sc, mt_sc, m_sc, acc_sc)

    lam = _lam(lq1_ref, lk1_ref, lq2_ref, lk2_ref, lam_init)
    gain = sg_ref[...] * (1.0 - lam_init)
    for hd in range(A_STEP_HEADS):
        lanes = slice(LANES * hd, LANES * (hd + 1))
        a0, a1 = acc_sc[2 * hd], acc_sc[2 * hd + 1]
        oa = ((a0[:A_V_DIM] / a0[A_V_DIM:A_V_DIM + 1])
              - lam * (a1[:A_V_DIM] / a1[A_V_DIM:A_V_DIM + 1]))
        on = oa * lax.rsqrt(jnp.mean(oa * oa, axis=0, keepdims=True) + EPS) * gain
        o_ref[:, lanes] = (on.T * _silu(z_ref[:, lanes])).astype(o_ref.dtype)


def _attn_a_prompt(qa_t, kab, va_t, za, lam_vecs, subln_col, lam_init):
    t = kab.shape[0]
    nh = A_STEP_HEADS
    vec = pl.BlockSpec((1, A_QK_DIM), lambda g, i: (0, 0))
    return pl.pallas_call(
        functools.partial(_attn_a_kernel, lam_init=lam_init),
        grid=(A_HEADS // nh, t // Q_TILE),
        in_specs=[pl.BlockSpec((nh * LANES, Q_TILE), lambda g, i: (g, i)),
                  _resident((t, nh * LANES), lambda g, i: (0, g)),
                  _resident((t // SLAB, nh * A_VT_ROWS, SLAB), lambda g, i: (0, g, 0)),
                  pl.BlockSpec((Q_TILE, nh * LANES), lambda g, i: (i, g)),
                  vec, vec, vec, vec,
                  pl.BlockSpec((A_V_DIM, 1), lambda g, i: (0, 0))],
        out_specs=pl.BlockSpec((Q_TILE, nh * LANES), lambda g, i: (i, g)),
        out_shape=jax.ShapeDtypeStruct((t, A_WIDTH), BF16),
        scratch_shapes=_sweep_scratch(A_VT_ROWS),
        compiler_params=_params(2),
        name="attn_a_prompt",
    )(qa_t, kab, va_t, za, *lam_vecs, subln_col)


def _attn_b_kernel(qt_ref, k_ref, vt_ref, z_ref, o_ref, s_sc, mt_sc, m_sc, acc_sc):
    qi = pl.program_id(1)
    shape = (KV_TILE, Q_TILE)
    causal = lax.broadcasted_iota(jnp.int32, shape, 0) <= lax.broadcasted_iota(jnp.int32, shape, 1)
    streams = []
    for t in range(B_STEP_HEADS):
        lanes = slice(LANES * t, LANES * (t + 1))
        rows = slice(B_VT_ROWS * t, B_VT_ROWS * (t + 1))
        streams.append((qt_ref[lanes, :],
                        functools.partial(_key_tile, k_ref, lanes=lanes),
                        lambda sl, rows=rows: vt_ref[sl, rows, :]))
    _kv_sweep(qi, streams, causal, s_sc, mt_sc, m_sc, acc_sc)
    for p in range(B_STEP_HEADS // 2):
        lanes = slice(LANES * p, LANES * (p + 1))
        a0, a1 = acc_sc[2 * p], acc_sc[2 * p + 1]
        o = jnp.concatenate([a0[:B_DIM] / a0[B_DIM:B_DIM + 1], a1[:B_DIM] / a1[B_DIM:B_DIM + 1]],
                            axis=0)
        o_ref[:, lanes] = (o.T * _silu(z_ref[:, lanes])).astype(o_ref.dtype)


def _attn_b_prompt(qb_t, kb_aug, vb_t, zb):
    t = kb_aug.shape[0]
    nh = B_STEP_HEADS
    out_w = nh * B_DIM
    return pl.pallas_call(
        _attn_b_kernel,
        grid=(B_HEADS // nh, t // Q_TILE),
        in_specs=[pl.BlockSpec((nh * LANES, Q_TILE), lambda g, i: (g, i)),
                  _resident((t, nh * LANES), lambda g, i: (0, g)),
                  _resident((t // SLAB, nh * B_VT_ROWS, SLAB), lambda g, i: (0, g, 0)),
                  pl.BlockSpec((Q_TILE, out_w), lambda g, i: (i, g))],
        out_specs=pl.BlockSpec((Q_TILE, out_w), lambda g, i: (i, g)),
        out_shape=jax.ShapeDtypeStruct((t, B_WIDTH), BF16),
        scratch_shapes=_sweep_scratch(B_VT_ROWS),
        compiler_params=_params(2),
        name="attn_b_prompt",
    )(qb_t, kb_aug, vb_t, zb)


def _softmax_two_part(s_c, s_n, pv_c, pv_n):
    m = jnp.maximum(jnp.max(s_c, axis=1, keepdims=True), jnp.max(s_n, axis=1, keepdims=True))
    p_c = jnp.exp(s_c - m)
    p_n = jnp.exp(s_n - m)
    acc = pv_c(p_c.astype(BF16)) + pv_n(p_n.astype(BF16))
    l = jnp.sum(p_c, axis=1, keepdims=True) + jnp.sum(p_n, axis=1, keepdims=True)
    return acc, l


def _diff_epilogue(o0, o1, lam, subln_g, lam_init, z):
    oa = o0 - lam * o1
    on = oa * lax.rsqrt(jnp.mean(oa * oa, axis=-1, keepdims=True) + EPS)
    return on * subln_g * (1.0 - lam_init) * _silu(z)


def _attn_dec_kernel(qa_ref, kan_ref, van_ref, za_ref, qb_ref, kbn_ref, vbn_ref, zb_ref,
                     cak_ref, cav_ref, cbk_ref, cbv_ref, lf_ref,
                     lq1_ref, lk1_ref, lq2_ref, lk2_ref, sg_ref,
                     ga_o, gb_o, *, lam_init):
    tdec = qa_ref.shape[0]
    past = cak_ref.shape[2]
    lane = lax.broadcasted_iota(jnp.int32, (tdec, LANES), 1)
    lam = _lam(lq1_ref, lk1_ref, lq2_ref, lk2_ref, lam_init)
    dot = functools.partial(jnp.dot, preferred_element_type=F32)
    dot_nt = lambda a, b: lax.dot_general(a, b, _NT, preferred_element_type=F32)

    for h in range(A_HEADS):
        cols = slice(LANES * h, LANES * (h + 1))
        q = qa_ref[:, cols]
        zero = jnp.zeros_like(q)
        kt_c = cak_ref[0, cols, :].astype(BF16)
        v_c = cav_ref[0, pl.ds(h, past, stride=A_HEADS), :].astype(BF16)
        k_n = kan_ref[:, cols]
        v_n = van_ref[:, cols]
        outs = []
        for t in range(2):
            qm = jnp.where((lane >= A_QK_DIM) == (t == 1), q, zero)
            acc, l = _softmax_two_part(dot(qm, kt_c), dot_nt(qm, k_n),
                                       lambda p: dot(p, v_c), lambda p: dot(p, v_n))
            outs.append(acc / l)
        ga_o[:, cols] = _diff_epilogue(outs[0], outs[1], lam, sg_ref[...], lam_init,
                                       za_ref[:, cols]).astype(ga_o.dtype)

    lf = lf_ref[0]
    pos = lax.broadcasted_iota(jnp.int32, lf.shape, 1)
    cum = lf
    s = 1
    while s < lf.shape[1]:
        cum = cum + jnp.where(pos >= s, pltpu.roll(cum, s, 1), 0.0)
        s *= 2
    row = lax.broadcasted_iota(jnp.int32, (tdec, tdec), 0)
    col = lax.broadcasted_iota(jnp.int32, (tdec, tdec), 1)
    for p in range(B_HEADS // 2):
        cols = slice(LANES * p, LANES * (p + 1))
        q = qb_ref[:, cols]
        zero = jnp.zeros_like(q)
        kt_c = cbk_ref[0, cols, :].astype(BF16)
        vt_c = cbv_ref[0, cols, :].astype(BF16)
        k_n = kbn_ref[:, cols]
        v_n = vbn_ref[:, cols]
        outs = []
        for t in range(2):
            qm = jnp.where((lane >= B_DIM) == (t == 1), q, zero)
            head = 2 * p + t
            s_c = dot(qm, kt_c) - cum[head:head + 1, :past]
            s_n = dot_nt(qm, k_n) - cum[head:head + 1, past:past + tdec]
            s_n = jnp.where(col <= row, s_n, -jnp.inf)
            acc, l = _softmax_two_part(s_c, s_n, lambda pr: dot_nt(pr, vt_c),
                                       lambda pr: dot(pr, v_n))
            outs.append(acc / l)
        o = jnp.where(lane < B_DIM, outs[0], outs[1])
        gb_o[:, cols] = (o * _silu(zb_ref[:, cols])).astype(gb_o.dtype)


def _attn_decode(proj, caches, lf_cat, lam_vecs, subln_g, lam_init, nb, tdec):
    cak_t, cav, cbk_t, cbv_t = caches
    past = cak_t.shape[2]
    rows = pl.BlockSpec((tdec, 512), lambda b: (b, 0))
    feat_major = pl.BlockSpec((1, 512, past), lambda b: (b, 0, 0))
    vec = pl.BlockSpec((1, A_QK_DIM), lambda b: (0, 0))
    return pl.pallas_call(
        functools.partial(_attn_dec_kernel, lam_init=lam_init),
        grid=(nb,),
        in_specs=[rows] * 8
                 + [feat_major, pl.BlockSpec((1, past * A_HEADS, A_V_DIM), lambda b: (b, 0, 0)),
                    feat_major, feat_major,
                    pl.BlockSpec((1, B_HEADS, lf_cat.shape[2]), lambda b: (b, 0, 0)),
                    vec, vec, vec, vec, pl.BlockSpec((1, A_V_DIM), lambda b: (0, 0))],
        out_specs=[rows, rows],
        out_shape=[jax.ShapeDtypeStruct((nb * tdec, A_WIDTH), BF16),
                   jax.ShapeDtypeStruct((nb * tdec, B_WIDTH), BF16)],
        compiler_params=_params(1),
        name="attn_decode",
    )(*proj, cak_t, cav, cbk_t, cbv_t, lf_cat, *lam_vecs, subln_g)


def _out_kernel(x_ref, gate_ref, ga_ref, gb_ref, w_ref, y_ref):
    o = (jnp.dot(ga_ref[...], w_ref[:A_WIDTH, :], preferred_element_type=F32)
         + jnp.dot(gb_ref[...], w_ref[A_WIDTH:, :], preferred_element_type=F32))
    y_ref[...] = x_ref[...] + gate_ref[...] * o


def _out_projection(x2, gate, ga, gb, w_out):
    t, d = x2.shape
    tm = min(OUT_ROWS, t)
    gate_spec = (pl.BlockSpec((tm, d), lambda i: (i, 0)) if gate.shape[0] != 1
                 else pl.BlockSpec((1, d), lambda i: (0, 0)))
    return pl.pallas_call(
        _out_kernel,
        grid=(t // tm,),
        in_specs=[pl.BlockSpec((tm, d), lambda i: (i, 0)), gate_spec,
                  pl.BlockSpec((tm, A_WIDTH), lambda i: (i, 0)),
                  pl.BlockSpec((tm, B_WIDTH), lambda i: (i, 0)),
                  pl.BlockSpec(w_out.shape, lambda i: (0, 0))],
        out_specs=pl.BlockSpec((tm, d), lambda i: (i, 0)),
        out_shape=jax.ShapeDtypeStruct((t, d), F32),
        compiler_params=_params(1),
        name="out_projection",
    )(x2, gate, ga, gb, w_out)


def _rope_tables(pos):
    inv = ROPE_THETA ** (-jnp.arange(0, ROT_DIM, 2, dtype=F32) / ROT_DIM)
    ang = (pos.astype(F32)[:, None] * inv[None, :]).T
    return jnp.cos(ang), jnp.sin(ang)


def _group_mean_matrix():
    idx = np.arange(MXU_DIM) // A_QK_DIM
    return jnp.asarray((idx[:, None] == idx[None, :]).astype(np.float32) / A_QK_DIM, dtype=BF16)


def kernel(x_prompt, x_sample, cache_a_k, cache_a_v, cache_b_k, cache_b_v, cache_b_logf,
           c_prompt, c_sample, norm_g, w_ada, b_ada, w_in, b_f, qn_a, kn_a,
           lam_q1, lam_k1, lam_q2, lam_k2, subln_g, qn_b, kn_b, w_out):
    depth = norm_g.shape[0]
    bp, seq, d = x_prompt.shape
    bs, tdec, _ = x_sample.shape
    past = cache_a_k.shape[2]
    assert bp == 1 and seq % KV_TILE == 0 and Q_TILE % CHUNK == 0 and seq % OUT_ROWS == 0
    assert past % CHUNK == 0 and tdec == CHUNK and (bs * tdec) % PROJ_ROWS == 0

    xp = x_prompt.reshape(seq, d)
    xs = x_sample.reshape(bs * tdec, d)
    c_rows = bp + bs
    c_all = jnp.concatenate([c_prompt, c_sample, jnp.zeros((16 - c_rows, d), F32)], axis=0)
    rope_p = _rope_tables(jnp.arange(seq))
    rope_s = _rope_tables(jnp.tile(past + jnp.arange(tdec), bs))
    gm = _group_mean_matrix()
    fb_off = 2 * A_QK_WIDTH + 2 * A_WIDTH + 3 * B_WIDTH

    outs_p = [[] for _ in range(5)]
    outs_s = [[] for _ in range(5)]
    for l in range(depth):
        lam_init = 0.8 - 0.6 * math.exp(-0.3 * l)
        w_all = w_in[l].astype(BF16)
        w_zb = w_all[:, fb_off + B_HEADS:]
        wf_t = jnp.concatenate([w_in[l][:, fb_off:fb_off + B_HEADS].T.astype(BF16),
                                jnp.zeros((16 - B_HEADS, d), BF16)], axis=0)
        bf_col = jnp.concatenate([b_f[l], jnp.zeros((16 - B_HEADS,), F32)]).reshape(16, 1)
        gains = tuple(jnp.tile(g[l], 512 // g.shape[1]).reshape(1, 512)
                      for g in (qn_a, kn_a, qn_b, kn_b))
        lam_vecs = tuple(v[l].reshape(1, A_QK_DIM) for v in (lam_q1, lam_k1, lam_q2, lam_k2))
        w_o = w_out[l].astype(BF16)
        g_row = norm_g[l].reshape(1, d)

        mod = _modulation(c_all, w_ada[l], b_ada[l])
        shift, scale, gate = mod[:, :d], mod[:, d:2 * d], mod[:, 2 * d:]

        (ka_t, va, za, kb, vb, zb, lf_t, qa_t, kab, va_t, qb_t, kb_aug, vb_t) = _projection(
            xp, scale[:bp], shift[:bp], g_row, w_all, w_zb, wf_t, bf_col, gains, rope_p, gm, True)
        ga = _attn_a_prompt(qa_t, kab, va_t, za, lam_vecs, subln_g[l].reshape(A_V_DIM, 1), lam_init)
        gb = _attn_b_prompt(qb_t, kb_aug, vb_t, zb)
        xp = _out_projection(xp, gate[:bp], ga, gb, w_o)
        ka = ka_t.reshape(A_HEADS, 2, A_QK_DIM, seq).transpose(3, 0, 1, 2)
        for dst, a in zip(outs_p, (ka.reshape(bp, seq, A_HEADS, 2, A_QK_DIM),
                                   va.reshape(bp, seq, A_HEADS, A_V_DIM),
                                   kb.reshape(bp, seq, B_HEADS, B_DIM),
                                   vb.reshape(bp, seq, B_HEADS, B_DIM),
                                   lf_t.T.reshape(bp, seq, B_HEADS))):
            dst.append(a)

        rep = lambda a: jnp.repeat(a[bp:c_rows], tdec, axis=0)
        (ka, va, za, kb, vb, zb, lf_t, qa, kab, vab, qb, kbb, vbb) = _projection(
            xs, rep(scale), rep(shift), g_row, w_all, w_zb, wf_t, bf_col, gains, rope_s, gm,
            False)
        lf_new = lf_t.reshape(B_HEADS, bs, tdec).transpose(1, 0, 2)
        lf_cat = jnp.concatenate([cache_b_logf[l].astype(F32).transpose(0, 2, 1), lf_new,
                                  jnp.zeros((bs, B_HEADS, LANES - tdec), F32)], axis=2)
        caches = (cache_a_k[l].transpose(0, 2, 3, 4, 1).reshape(bs, A_QK_WIDTH, past),
                  cache_a_v[l].reshape(bs, past * A_HEADS, A_V_DIM),
                  cache_b_k[l].transpose(0, 2, 3, 1).reshape(bs, B_WIDTH, past),
                  cache_b_v[l].transpose(0, 2, 3, 1).reshape(bs, B_WIDTH, past))
        ga, gb = _attn_decode((qa, kab, vab, za, qb, kbb, vbb, zb), caches, lf_cat,
                              lam_vecs, subln_g[l].reshape(1, A_V_DIM), lam_init, bs, tdec)
        xs = _out_projection(xs, rep(gate), ga, gb, w_o)
        for dst, a in zip(outs_s, (ka.reshape(bs, tdec, A_HEADS, 2, A_QK_DIM),
                                   va.reshape(bs, tdec, A_HEADS, A_V_DIM),
                                   kb.reshape(bs, tdec, B_HEADS, B_DIM),
                                   vb.reshape(bs, tdec, B_HEADS, B_DIM),
                                   lf_t.T.reshape(bs, tdec, B_HEADS))):
            dst.append(a)

    return (xp.reshape(bp, seq, d), xs.reshape(bs, tdec, d),
            *(jnp.stack(o) for o in outs_p), *(jnp.stack(o) for o in outs_s))
```

```python
import functools
import math

import numpy as np
import jax
import jax.numpy as jnp
from jax import lax
from jax.experimental import pallas as pl
from jax.experimental.pallas import tpu as pltpu

F32 = jnp.float32
BF16 = jnp.bfloat16

CHUNK = 64
A_HEADS = 4
A_QK_DIM = 64
A_V_DIM = 2 * A_QK_DIM
B_HEADS = 8
B_DIM = 64
A_WIDTH = A_HEADS * A_V_DIM
B_WIDTH = B_HEADS * B_DIM
A_QK_WIDTH = A_HEADS * 2 * A_QK_DIM
ROT_DIM = A_QK_DIM // 4
ROPE_THETA = 500000.0
EPS = 1e-6
LOG2_E = math.log2(math.e)

LANES = 128
BF16_ROWS = 16
MXU_DIM = 256
VMEM_LIMIT = 56 * 1024 * 1024

SLAB = MXU_DIM
PROJ_ROWS = SLAB
KV_TILE = 2 * SLAB
Q_TILE = KV_TILE
STEP_STREAMS = 4
A_STEP_HEADS = STEP_STREAMS // 2
B_STEP_HEADS = STEP_STREAMS
OUT_ROWS = 1024

A_VT_ROWS = A_V_DIM + BF16_ROWS
B_VT_ROWS = B_DIM + BF16_ROWS
B_BIAS_PIECES = 3
N_GROUPS = 8

_NT = (((1,), (1,)), ((), ()))


def _params(n_axes):
    return pltpu.CompilerParams(dimension_semantics=("arbitrary",) * n_axes,
                                vmem_limit_bytes=VMEM_LIMIT)


def _silu(x):
    return x * jax.nn.sigmoid(x)


def _mod_kernel(c_ref, w_ref, b_ref, o_ref):
    a = _silu(c_ref[...]).astype(BF16)
    o_ref[...] = jnp.dot(a, w_ref[...].astype(BF16), preferred_element_type=F32) + b_ref[...]


def _modulation(c_all, w_ada, b_ada):
    rows, d = c_all.shape
    n = w_ada.shape[1]
    bn = 512
    return pl.pallas_call(
        _mod_kernel,
        grid=(n // bn,),
        in_specs=[pl.BlockSpec((rows, d), lambda j: (0, 0)),
                  pl.BlockSpec((d, bn), lambda j: (0, j)),
                  pl.BlockSpec((1, bn), lambda j: (0, j))],
        out_specs=pl.BlockSpec((rows, bn), lambda j: (0, j)),
        out_shape=jax.ShapeDtypeStruct((rows, n), F32),
        compiler_params=_params(1),
        name="modulation",
    )(c_all, w_ada, b_ada.reshape(1, n))


def _group_rms(u, gm):
    sq = u * u
    hi = sq.astype(BF16)
    lo = (sq - hi.astype(F32)).astype(BF16)
    parts = []
    for c in range(u.shape[1] // MXU_DIM):
        sl = slice(MXU_DIM * c, MXU_DIM * (c + 1))
        parts.append(jnp.dot(hi[:, sl], gm, preferred_element_type=F32)
                     + jnp.dot(lo[:, sl], gm, preferred_element_type=F32))
    ms = jnp.concatenate(parts, axis=1)
    return u * lax.rsqrt(ms + EPS)


def _rope_t(xt, cos_t, sin_t):
    half = ROT_DIM // 2
    pieces = []
    for base in range(0, xt.shape[0], A_QK_DIM):
        x1 = xt[base:base + half]
        x2 = xt[base + half:base + ROT_DIM]
        pieces += [x1 * cos_t - x2 * sin_t, x2 * cos_t + x1 * sin_t,
                   xt[base + ROT_DIM:base + A_QK_DIM]]
    return jnp.concatenate(pieces, axis=0)


def _with_ones_rows(xt, rows, n_ones):
    r = lax.broadcasted_iota(jnp.int32, (rows, xt.shape[1]), 0)
    return jnp.concatenate([xt, jnp.where(r < n_ones, 1.0, 0.0)], axis=0)


def _proj_kernel(x_ref, scale_ref, shift_ref, g_ref, w_ref, wz_ref, wf_ref, bf_ref,
                 qna_ref, kna_ref, qnb_ref, knb_ref, cos_ref, sin_ref, gm_ref,
                 ka_o, va_o, za_o, kb_o, vb_o, zb_o, lf_o,
                 qa_o, kab_o, vab_o, qb_o, kbb_o, vbb_o, carry_ref, *, transposed):
    @pl.when(pl.program_id(0) == 0)
    def _():
        carry_ref[...] = jnp.zeros_like(carry_ref)

    tm = x_ref.shape[0]
    x = x_ref[...]
    y = x * lax.rsqrt(jnp.mean(x * x, axis=-1, keepdims=True) + EPS) * g_ref[...]
    h = y * (1.0 + scale_ref[...]) + shift_ref[...]
    hb = h.astype(BF16)
    gm = gm_ref[...]
    cos_t, sin_t = cos_ref[...], sin_ref[...]
    qk_scale = A_QK_DIM ** -0.5 * (LOG2_E if transposed else 1.0)
    lane = lax.broadcasted_iota(jnp.int32, (tm, LANES), 1)

    def group(i):
        w = wz_ref[...] if i == N_GROUPS - 1 else w_ref[:, 512 * i:512 * (i + 1)]
        return jnp.dot(hb, w, preferred_element_type=F32)

    fb = lax.dot_general(wf_ref[...], hb, _NT, preferred_element_type=F32)
    fx = fb + bf_ref[...]
    logf = jnp.minimum(fx, 0.0) - jnp.log1p(jnp.exp(-jnp.abs(fx)))
    lf_o[...] = logf[:B_HEADS]

    qa_t = _rope_t((_group_rms(group(0), gm) * qna_ref[...]).T, cos_t, sin_t) * qk_scale
    ka_t = _rope_t((_group_rms(group(1), gm) * kna_ref[...]).T, cos_t, sin_t)
    ka = ka_t.T
    ka_o[...] = ka_t if transposed else ka
    kab_o[...] = ka.astype(BF16)
    va = group(2)
    for hd in range(A_HEADS):
        va_o[pl.ds(hd, tm, stride=A_HEADS), :] = va[:, A_V_DIM * hd:A_V_DIM * (hd + 1)]
    za_o[...] = group(3)
    qb = _group_rms(group(4), gm) * qnb_ref[...] * qk_scale
    kb = _group_rms(group(5), gm) * knb_ref[...]
    kb_o[...] = kb
    vb = group(6)
    vb_o[...] = vb
    zb_o[...] = group(7)

    if not transposed:
        qa_o[...] = qa_t.T.astype(BF16)
        vab_o[...] = va.astype(BF16)
        qb_o[...] = qb.astype(BF16)
        kbb_o[...] = kb.astype(BF16)
        vbb_o[...] = vb.astype(BF16)
        return

    qa_o[...] = qa_t.astype(BF16)
    va_t = va.T
    for hd in range(A_HEADS):
        blk = _with_ones_rows(va_t[A_V_DIM * hd:A_V_DIM * (hd + 1)], BF16_ROWS, 1)
        vab_o[0, A_VT_ROWS * hd:A_VT_ROWS * (hd + 1), :] = blk.astype(BF16)
    qb_t = qb.T
    vb_t = vb.T
    for hd in range(B_HEADS):
        blk = _with_ones_rows(qb_t[B_DIM * hd:B_DIM * (hd + 1)], LANES - B_DIM, B_BIAS_PIECES)
        qb_o[LANES * hd:LANES * (hd + 1), :] = blk.astype(BF16)
        blk = _with_ones_rows(vb_t[B_DIM * hd:B_DIM * (hd + 1)], BF16_ROWS, 1)
        vbb_o[0, B_VT_ROWS * hd:B_VT_ROWS * (hd + 1), :] = blk.astype(BF16)

    pos = lax.broadcasted_iota(jnp.int32, logf.shape, 1)
    c = logf
    s = 1
    while s < tm:
        c = c + jnp.where(pos >= s, pltpu.roll(c, s, 1), 0.0)
        s *= 2
    c = c + carry_ref[...]
    carry_ref[...] = c[:, tm - 1:tm]

    cum_rows = jnp.concatenate([c, jnp.zeros((LANES - c.shape[0], tm), F32)], axis=0).T
    for hd in range(B_HEADS):
        blk = kb[:, LANES * (hd // 2):LANES * (hd // 2 + 1)]
        if hd % 2:
            blk = pltpu.roll(blk, B_DIM, 1)
        neg = -LOG2_E * jnp.broadcast_to(cum_rows[:, hd:hd + 1], (tm, LANES))
        hi = neg.astype(BF16).astype(F32)
        rest = neg - hi
        mid = rest.astype(BF16).astype(F32)
        lo = rest - mid
        aug = jnp.where(lane < B_DIM, blk,
                        jnp.where(lane == B_DIM, hi,
                                  jnp.where(lane == B_DIM + 1, mid,
                                            jnp.where(lane == B_DIM + 2, lo, 0.0))))
        kbb_o[:, LANES * hd:LANES * (hd + 1)] = aug.astype(BF16)


def _projection(x2, scale, shift, norm_g, w_all, w_zb, wf_t, bf_col, gains, rope_tabs, gm,
                transposed):
    t, d = x2.shape
    tm = PROJ_ROWS
    per_row = scale.shape[0] != 1
    mod_spec = (pl.BlockSpec((tm, d), lambda i: (i, 0)) if per_row
                else pl.BlockSpec((1, d), lambda i: (0, 0)))
    const = lambda shape: pl.BlockSpec(shape, lambda i: (0, 0))
    row = lambda width: pl.BlockSpec((tm, width), lambda i: (i, 0))
    col = lambda height: pl.BlockSpec((height, tm), lambda i: (0, i))
    slab = lambda height: pl.BlockSpec((1, height, tm), lambda i: (i, 0, 0))
    in_specs = [row(d), mod_spec, mod_spec, const((1, d)),
                const(w_all.shape), const(w_zb.shape), const(wf_t.shape), const(bf_col.shape),
                const((1, 512)), const((1, 512)), const((1, 512)), const((1, 512)),
                col(ROT_DIM // 2), col(ROT_DIM // 2), const(gm.shape)]
    f32o = jax.ShapeDtypeStruct((t, 512), F32)
    bf = lambda *shape: jax.ShapeDtypeStruct(shape, BF16)
    va_shape = jax.ShapeDtypeStruct((t * A_HEADS, A_V_DIM), F32)
    va_spec = pl.BlockSpec((tm * A_HEADS, A_V_DIM), lambda i: (i, 0))
    ka_shape = jax.ShapeDtypeStruct((512, t), F32) if transposed else f32o
    ka_spec = col(512) if transposed else row(512)
    out_shape = [ka_shape, va_shape] + [f32o] * 4 + [jax.ShapeDtypeStruct((B_HEADS, t), F32)]
    out_specs = [ka_spec, va_spec] + [row(512)] * 4 + [col(B_HEADS)]
    if transposed:
        out_shape += [bf(512, t), bf(t, 512), bf(t // tm, A_HEADS * A_VT_ROWS, tm),
                      bf(B_HEADS * LANES, t), bf(t, B_HEADS * LANES),
                      bf(t // tm, B_HEADS * B_VT_ROWS, tm)]
        out_specs += [col(512), row(512), slab(A_HEADS * A_VT_ROWS),
                      col(B_HEADS * LANES), row(B_HEADS * LANES), slab(B_HEADS * B_VT_ROWS)]
    else:
        out_shape += [bf(t, 512)] * 6
        out_specs += [row(512)] * 6
    return pl.pallas_call(
        functools.partial(_proj_kernel, transposed=transposed),
        grid=(t // tm,),
        in_specs=in_specs,
        out_specs=out_specs,
        out_shape=out_shape,
        scratch_shapes=[pltpu.VMEM((16, 1), F32)],
        compiler_params=_params(1),
        name="projection_prompt" if transposed else "projection_decode",
    )(x2, scale, shift, norm_g, w_all, w_zb, wf_t, bf_col, *gains, *rope_tabs, gm)


def _col_max(s):
    groups = s.reshape(8, s.shape[0] // 8, s.shape[1])
    return jnp.max(jnp.max(groups, axis=0), axis=0, keepdims=True)


def _kv_sweep(n_full, streams, last_mask, s_sc, mt_sc, m_sc, acc_sc):
    def produce(j, slot):
        for t, (q_cols, k_of, _) in enumerate(streams):
            s = jnp.dot(k_of(j), q_cols, preferred_element_type=F32)
            s_sc[t, slot] = s
            mt_sc[t, slot] = _col_max(s)

    def consume(j, slot, mask):
        for t, (_, _, v_of) in enumerate(streams):
            s = s_sc[t, slot]
            if mask is None:
                tile_max = mt_sc[t, slot]
            else:
                s = jnp.where(mask, s, -jnp.inf)
                tile_max = _col_max(s)
            m_prev = m_sc[t]
            m_new = jnp.maximum(m_prev, tile_max)
            p = jnp.exp2(s - m_new).astype(BF16)
            pv = None
            for c in range(KV_TILE // SLAB):
                part = jnp.dot(v_of(j * (KV_TILE // SLAB) + c), p[SLAB * c:SLAB * (c + 1)],
                               preferred_element_type=F32)
                pv = part if pv is None else pv + part
            acc_sc[t] = jnp.exp2(m_prev - m_new) * acc_sc[t] + pv
            m_sc[t] = m_new

    m_sc[...] = jnp.full(m_sc.shape, -jnp.inf, F32)
    acc_sc[...] = jnp.zeros_like(acc_sc)
    produce(0, 0)

    def body(i, carry):
        produce(2 * i + 1, 1)
        consume(2 * i, 0, None)
        produce(2 * i + 2, 0)
        consume(2 * i + 1, 1, None)
        return carry

    lax.fori_loop(0, n_full // 2, body, 0)

    @pl.when(n_full % 2 == 1)
    def _():
        produce(n_full, 1)
        consume(n_full - 1, 0, None)
        consume(n_full, 1, last_mask)

    @pl.when(n_full % 2 == 0)
    def _():
        consume(n_full, 0, last_mask)


def _sweep_scratch(rows):
    n = STEP_STREAMS
    return [pltpu.VMEM((n, 2, KV_TILE, Q_TILE), F32), pltpu.VMEM((n, 2, 1, Q_TILE), F32),
            pltpu.VMEM((n, 1, Q_TILE), F32), pltpu.VMEM((n, rows, Q_TILE), F32)]


def _resident(block_shape, index_map, single):
    mode = pl.Buffered(1) if single else None
    return pl.BlockSpec(block_shape, index_map, pipeline_mode=mode)


def _key_tile(k_ref, j, lanes):
    return k_ref[pl.ds(pl.multiple_of(j * KV_TILE, KV_TILE), KV_TILE), lanes]


def _lam(lq1_ref, lk1_ref, lq2_ref, lk2_ref, lam_init):
    return (jnp.exp(jnp.sum(lq1_ref[...] * lk1_ref[...], axis=1, keepdims=True))
            - jnp.exp(jnp.sum(lq2_ref[...] * lk2_ref[...], axis=1, keepdims=True)) + lam_init)


def _attn_a_kernel(qt_ref, k_ref, vt_ref, z_ref, lq1_ref, lk1_ref, lq2_ref, lk2_ref, sg_ref,
                   o_ref, s_sc, mt_sc, m_sc, acc_sc, *, lam_init):
    qi = pl.program_id(1)
    row = lax.broadcasted_iota(jnp.int32, (LANES, Q_TILE), 0)
    streams = []
    for hd in range(A_STEP_HEADS):
        lanes = slice(LANES * hd, LANES * (hd + 1))
        rows = slice(A_VT_ROWS * hd, A_VT_ROWS * (hd + 1))
        qt = qt_ref[lanes, :]
        zero = jnp.zeros_like(qt)
        k_of = functools.partial(_key_tile, k_ref, lanes=lanes)
        v_of = lambda sl, rows=rows: vt_ref[sl, rows, :]
        streams.append((jnp.where(row < A_QK_DIM, qt, zero), k_of, v_of))
        streams.append((jnp.where(row >= A_QK_DIM, qt, zero), k_of, v_of))
    shape = (KV_TILE, Q_TILE)
    chunk_mask = (lax.broadcasted_iota(jnp.int32, shape, 0) // CHUNK
                  <= lax.broadcasted_iota(jnp.int32, shape, 1) // CHUNK)
    _kv_sweep(qi, streams, chunk_mask, s_sc, mt_sc, m_sc, acc_sc)

    lam = _lam(lq1_ref, lk1_ref, lq2_ref, lk2_ref, lam_init)
    gain = sg_ref[...] * (1.0 - lam_init)
    for hd in range(A_STEP_HEADS):
        lanes = slice(LANES * hd, LANES * (hd + 1))
        a0, a1 = acc_sc[2 * hd], acc_sc[2 * hd + 1]
        oa = ((a0[:A_V_DIM] / a0[A_V_DIM:A_V_DIM + 1])
              - lam * (a1[:A_V_DIM] / a1[A_V_DIM:A_V_DIM + 1]))
        on = oa * lax.rsqrt(jnp.mean(oa * oa, axis=0, keepdims=True) + EPS) * gain
        o_ref[:, lanes] = (on.T * _silu(z_ref[:, lanes])).astype(o_ref.dtype)


def _attn_a_prompt(qa_t, kab, va_t, za, lam_vecs, subln_col, lam_init):
    t = kab.shape[0]
    nh = A_STEP_HEADS
    vec = pl.BlockSpec((1, A_QK_DIM), lambda g, i: (0, 0))
    return pl.pallas_call(
        functools.partial(_attn_a_kernel, lam_init=lam_init),
        grid=(A_HEADS // nh, t // Q_TILE),
        in_specs=[pl.BlockSpec((nh * LANES, Q_TILE), lambda g, i: (g, i)),
                  _resident((t, nh * LANES), lambda g, i: (0, g), False),
                  _resident((t // SLAB, nh * A_VT_ROWS, SLAB), lambda g, i: (0, g, 0), False),
                  pl.BlockSpec((Q_TILE, nh * LANES), lambda g, i: (i, g)),
                  vec, vec, vec, vec,
                  pl.BlockSpec((A_V_DIM, 1), lambda g, i: (0, 0))],
        out_specs=pl.BlockSpec((Q_TILE, nh * LANES), lambda g, i: (i, g)),
        out_shape=jax.ShapeDtypeStruct((t, A_WIDTH), BF16),
        scratch_shapes=_sweep_scratch(A_VT_ROWS),
        compiler_params=_params(2),
        name="attn_a_prompt",
    )(qa_t, kab, va_t, za, *lam_vecs, subln_col)


def _attn_b_kernel(qt_ref, k_ref, vt_ref, z_ref, o_ref, s_sc, mt_sc, m_sc, acc_sc):
    qi = pl.program_id(1)
    shape = (KV_TILE, Q_TILE)
    causal = lax.broadcasted_iota(jnp.int32, shape, 0) <= lax.broadcasted_iota(jnp.int32, shape, 1)
    streams = []
    for t in range(B_STEP_HEADS):
        lanes = slice(LANES * t, LANES * (t + 1))
        rows = slice(B_VT_ROWS * t, B_VT_ROWS * (t + 1))
        streams.append((qt_ref[lanes, :],
                        functools.partial(_key_tile, k_ref, lanes=lanes),
                        lambda sl, rows=rows: vt_ref[sl, rows, :]))
    _kv_sweep(qi, streams, causal, s_sc, mt_sc, m_sc, acc_sc)
    for p in range(B_STEP_HEADS // 2):
        lanes = slice(LANES * p, LANES * (p + 1))
        a0, a1 = acc_sc[2 * p], acc_sc[2 * p + 1]
        o = jnp.concatenate([a0[:B_DIM] / a0[B_DIM:B_DIM + 1], a1[:B_DIM] / a1[B_DIM:B_DIM + 1]],
                            axis=0)
        o_ref[:, lanes] = (o.T * _silu(z_ref[:, lanes])).astype(o_ref.dtype)


def _attn_b_prompt(qb_t, kb_aug, vb_t, zb):
    t = kb_aug.shape[0]
    nh = B_STEP_HEADS
    out_w = nh * B_DIM
    return pl.pallas_call(
        _attn_b_kernel,
        grid=(B_HEADS // nh, t // Q_TILE),
        in_specs=[pl.BlockSpec((nh * LANES, Q_TILE), lambda g, i: (g, i)),
                  _resident((t, nh * LANES), lambda g, i: (0, g), True),
                  _resident((t // SLAB, nh * B_VT_ROWS, SLAB), lambda g, i: (0, g, 0), True),
                  pl.BlockSpec((Q_TILE, out_w), lambda g, i: (i, g))],
        out_specs=pl.BlockSpec((Q_TILE, out_w), lambda g, i: (i, g)),
        out_shape=jax.ShapeDtypeStruct((t, B_WIDTH), BF16),
        scratch_shapes=_sweep_scratch(B_VT_ROWS),
        compiler_params=_params(2),
        name="attn_b_prompt",
    )(qb_t, kb_aug, vb_t, zb)


def _softmax_two_part(s_c, s_n, pv_c, pv_n):
    m = jnp.maximum(jnp.max(s_c, axis=1, keepdims=True), jnp.max(s_n, axis=1, keepdims=True))
    p_c = jnp.exp(s_c - m)
    p_n = jnp.exp(s_n - m)
    acc = pv_c(p_c.astype(BF16)) + pv_n(p_n.astype(BF16))
    l = jnp.sum(p_c, axis=1, keepdims=True) + jnp.sum(p_n, axis=1, keepdims=True)
    return acc, l


def _diff_epilogue(o0, o1, lam, subln_g, lam_init, z):
    oa = o0 - lam * o1
    on = oa * lax.rsqrt(jnp.mean(oa * oa, axis=-1, keepdims=True) + EPS)
    return on * subln_g * (1.0 - lam_init) * _silu(z)


def _attn_dec_kernel(qa_ref, kan_ref, van_ref, za_ref, qb_ref, kbn_ref, vbn_ref, zb_ref,
                     cak_ref, cav_ref, cbk_ref, cbv_ref, lf_ref,
                     lq1_ref, lk1_ref, lq2_ref, lk2_ref, sg_ref,
                     ga_o, gb_o, *, lam_init):
    tdec = qa_ref.shape[0]
    past = cak_ref.shape[2]
    lane = lax.broadcasted_iota(jnp.int32, (tdec, LANES), 1)
    lam = _lam(lq1_ref, lk1_ref, lq2_ref, lk2_ref, lam_init)
    dot = functools.partial(jnp.dot, preferred_element_type=F32)
    dot_nt = lambda a, b: lax.dot_general(a, b, _NT, preferred_element_type=F32)

    for h in range(A_HEADS):
        cols = slice(LANES * h, LANES * (h + 1))
        q = qa_ref[:, cols]
        zero = jnp.zeros_like(q)
        kt_c = cak_ref[0, cols, :].astype(BF16)
        v_c = cav_ref[0, pl.ds(h, past, stride=A_HEADS), :].astype(BF16)
        k_n = kan_ref[:, cols]
        v_n = van_ref[:, cols]
        outs = []
        for t in range(2):
            qm = jnp.where((lane >= A_QK_DIM) == (t == 1), q, zero)
            acc, l = _softmax_two_part(dot(qm, kt_c), dot_nt(qm, k_n),
                                       lambda p: dot(p, v_c), lambda p: dot(p, v_n))
            outs.append(acc / l)
        ga_o[:, cols] = _diff_epilogue(outs[0], outs[1], lam, sg_ref[...], lam_init,
                                       za_ref[:, cols]).astype(ga_o.dtype)

    lf = lf_ref[0]
    pos = lax.broadcasted_iota(jnp.int32, lf.shape, 1)
    cum = lf
    s = 1
    while s < lf.shape[1]:
        cum = cum + jnp.where(pos >= s, pltpu.roll(cum, s, 1), 0.0)
        s *= 2
    row = lax.broadcasted_iota(jnp.int32, (tdec, tdec), 0)
    col = lax.broadcasted_iota(jnp.int32, (tdec, tdec), 1)
    for p in range(B_HEADS // 2):
        cols = slice(LANES * p, LANES * (p + 1))
        q = qb_ref[:, cols]
        zero = jnp.zeros_like(q)
        kt_c = cbk_ref[0, cols, :].astype(BF16)
        vt_c = cbv_ref[0, cols, :].astype(BF16)
        k_n = kbn_ref[:, cols]
        v_n = vbn_ref[:, cols]
        outs = []
        for t in range(2):
            qm = jnp.where((lane >= B_DIM) == (t == 1), q, zero)
            head = 2 * p + t
            s_c = dot(qm, kt_c) - cum[head:head + 1, :past]
            s_n = dot_nt(qm, k_n) - cum[head:head + 1, past:past + tdec]
            s_n = jnp.where(col <= row, s_n, -jnp.inf)
            acc, l = _softmax_two_part(s_c, s_n, lambda pr: dot_nt(pr, vt_c),
                                       lambda pr: dot(pr, v_n))
            outs.append(acc / l)
        o = jnp.where(lane < B_DIM, outs[0], outs[1])
        gb_o[:, cols] = (o * _silu(zb_ref[:, cols])).astype(gb_o.dtype)


def _attn_decode(proj, caches, lf_cat, lam_vecs, subln_g, lam_init, nb, tdec):
    cak_t, cav, cbk_t, cbv_t = caches
    past = cak_t.shape[2]
    rows = pl.BlockSpec((tdec, 512), lambda b: (b, 0))
    feat_major = pl.BlockSpec((1, 512, past), lambda b: (b, 0, 0))
    vec = pl.BlockSpec((1, A_QK_DIM), lambda b: (0, 0))
    return pl.pallas_call(
        functools.partial(_attn_dec_kernel, lam_init=lam_init),
        grid=(nb,),
        in_specs=[rows] * 8
                 + [feat_major, pl.BlockSpec((1, past * A_HEADS, A_V_DIM), lambda b: (b, 0, 0)),
                    feat_major, feat_major,
                    pl.BlockSpec((1, B_HEADS, lf_cat.shape[2]), lambda b: (b, 0, 0)),
                    vec, vec, vec, vec, pl.BlockSpec((1, A_V_DIM), lambda b: (0, 0))],
        out_specs=[rows, rows],
        out_shape=[jax.ShapeDtypeStruct((nb * tdec, A_WIDTH), BF16),
                   jax.ShapeDtypeStruct((nb * tdec, B_WIDTH), BF16)],
        compiler_params=_params(1),
        name="attn_decode",
    )(*proj, cak_t, cav, cbk_t, cbv_t, lf_cat, *lam_vecs, subln_g)


def _out_kernel(x_ref, gate_ref, ga_ref, gb_ref, w_ref, y_ref):
    o = (jnp.dot(ga_ref[...], w_ref[:A_WIDTH, :], preferred_element_type=F32)
         + jnp.dot(gb_ref[...], w_ref[A_WIDTH:, :], preferred_element_type=F32))
    y_ref[...] = x_ref[...] + gate_ref[...] * o


def _out_projection(x2, gate, ga, gb, w_out):
    t, d = x2.shape
    tm = min(OUT_ROWS, t)
    gate_spec = (pl.BlockSpec((tm, d), lambda i: (i, 0)) if gate.shape[0] != 1
                 else pl.BlockSpec((1, d), lambda i: (0, 0)))
    return pl.pallas_call(
        _out_kernel,
        grid=(t // tm,),
        in_specs=[pl.BlockSpec((tm, d), lambda i: (i, 0)), gate_spec,
                  pl.BlockSpec((tm, A_WIDTH), lambda i: (i, 0)),
                  pl.BlockSpec((tm, B_WIDTH), lambda i: (i, 0)),
                  pl.BlockSpec(w_out.shape, lambda i: (0, 0))],
        out_specs=pl.BlockSpec((tm, d), lambda i: (i, 0)),
        out_shape=jax.ShapeDtypeStruct((t, d), F32),
        compiler_params=_params(1),
        name="out_projection",
    )(x2, gate, ga, gb, w_out)


def _rope_tables(pos):
    inv = ROPE_THETA ** (-jnp.arange(0, ROT_DIM, 2, dtype=F32) / ROT_DIM)
    ang = (pos.astype(F32)[:, None] * inv[None, :]).T
    return jnp.cos(ang), jnp.sin(ang)


def _group_mean_matrix():
    idx = np.arange(MXU_DIM) // A_QK_DIM
    return jnp.asarray((idx[:, None] == idx[None, :]).astype(np.float32) / A_QK_DIM, dtype=BF16)


def kernel(x_prompt, x_sample, cache_a_k, cache_a_v, cache_b_k, cache_b_v, cache_b_logf,
           c_prompt, c_sample, norm_g, w_ada, b_ada, w_in, b_f, qn_a, kn_a,
           lam_q1, lam_k1, lam_q2, lam_k2, subln_g, qn_b, kn_b, w_out):
    depth = norm_g.shape[0]
    bp, seq, d = x_prompt.shape
    bs, tdec, _ = x_sample.shape
    past = cache_a_k.shape[2]
    assert bp == 1 and seq % KV_TILE == 0 and Q_TILE % CHUNK == 0 and seq % OUT_ROWS == 0
    assert past % CHUNK == 0 and tdec == CHUNK and (bs * tdec) % PROJ_ROWS == 0

    xp = x_prompt.reshape(seq, d)
    xs = x_sample.reshape(bs * tdec, d)
    c_rows = bp + bs
    c_all = jnp.concatenate([c_prompt, c_sample, jnp.zeros((16 - c_rows, d), F32)], axis=0)
    rope_p = _rope_tables(jnp.arange(seq))
    rope_s = _rope_tables(jnp.tile(past + jnp.arange(tdec), bs))
    gm = _group_mean_matrix()
    fb_off = 2 * A_QK_WIDTH + 2 * A_WIDTH + 3 * B_WIDTH

    outs_p = [[] for _ in range(5)]
    outs_s = [[] for _ in range(5)]
    for l in range(depth):
        lam_init = 0.8 - 0.6 * math.exp(-0.3 * l)
        w_all = w_in[l].astype(BF16)
        w_zb = w_all[:, fb_off + B_HEADS:]
        wf_t = jnp.concatenate([w_in[l][:, fb_off:fb_off + B_HEADS].T.astype(BF16),
                                jnp.zeros((16 - B_HEADS, d), BF16)], axis=0)
        bf_col = jnp.concatenate([b_f[l], jnp.zeros((16 - B_HEADS,), F32)]).reshape(16, 1)
        gains = tuple(jnp.tile(g[l], 512 // g.shape[1]).reshape(1, 512)
                      for g in (qn_a, kn_a, qn_b, kn_b))
        lam_vecs = tuple(v[l].reshape(1, A_QK_DIM) for v in (lam_q1, lam_k1, lam_q2, lam_k2))
        w_o = w_out[l].astype(BF16)
        g_row = norm_g[l].reshape(1, d)

        mod = _modulation(c_all, w_ada[l], b_ada[l])
        shift, scale, gate = mod[:, :d], mod[:, d:2 * d], mod[:, 2 * d:]

        (ka_t, va, za, kb, vb, zb, lf_t, qa_t, kab, va_t, qb_t, kb_aug, vb_t) = _projection(
            xp, scale[:bp], shift[:bp], g_row, w_all, w_zb, wf_t, bf_col, gains, rope_p, gm, True)
        ga = _attn_a_prompt(qa_t, kab, va_t, za, lam_vecs, subln_g[l].reshape(A_V_DIM, 1), lam_init)
        gb = _attn_b_prompt(qb_t, kb_aug, vb_t, zb)
        xp = _out_projection(xp, gate[:bp], ga, gb, w_o)
        ka = ka_t.reshape(A_HEADS, 2, A_QK_DIM, seq).transpose(3, 0, 1, 2)
        for dst, a in zip(outs_p, (ka.reshape(bp, seq, A_HEADS, 2, A_QK_DIM),
                                   va.reshape(bp, seq, A_HEADS, A_V_DIM),
                                   kb.reshape(bp, seq, B_HEADS, B_DIM),
                                   vb.reshape(bp, seq, B_HEADS, B_DIM),
                                   lf_t.T.reshape(bp, seq, B_HEADS))):
            dst.append(a)

        rep = lambda a: jnp.repeat(a[bp:c_rows], tdec, axis=0)
        (ka, va, za, kb, vb, zb, lf_t, qa, kab, vab, qb, kbb, vbb) = _projection(
            xs, rep(scale), rep(shift), g_row, w_all, w_zb, wf_t, bf_col, gains, rope_s, gm,
            False)
        lf_new = lf_t.reshape(B_HEADS, bs, tdec).transpose(1, 0, 2)
        lf_cat = jnp.concatenate([cache_b_logf[l].astype(F32).transpose(0, 2, 1), lf_new,
                                  jnp.zeros((bs, B_HEADS, LANES - tdec), F32)], axis=2)
        caches = (cache_a_k[l].transpose(0, 2, 3, 4, 1).reshape(bs, A_QK_WIDTH, past),
                  cache_a_v[l].reshape(bs, past * A_HEADS, A_V_DIM),
                  cache_b_k[l].transpose(0, 2, 3, 1).reshape(bs, B_WIDTH, past),
                  cache_b_v[l].transpose(0, 2, 3, 1).reshape(bs, B_WIDTH, past))
        ga, gb = _attn_decode((qa, kab, vab, za, qb, kbb, vbb, zb), caches, lf_cat,
                              lam_vecs, subln_g[l].reshape(1, A_V_DIM), lam_init, bs, tdec)
        xs = _out_projection(xs, rep(gate), ga, gb, w_o)
        for dst, a in zip(outs_s, (ka.reshape(bs, tdec, A_HEADS, 2, A_QK_DIM),
                                   va.reshape(bs, tdec, A_HEADS, A_V_DIM),
                                   kb.reshape(bs, tdec, B_HEADS, B_DIM),
                                   vb.reshape(bs, tdec, B_HEADS, B_DIM),
                                   lf_t.T.reshape(bs, tdec, B_HEADS))):
            dst.append(a)

    return (xp.reshape(bp, seq, d), xs.reshape(bs, tdec, d),
            *(jnp.stack(o) for o in outs_p), *(jnp.stack(o) for o in outs_s))
```

```python
import functools
import math

import numpy as np
import jax
import jax.numpy as jnp
from jax import lax
from jax.experimental import pallas as pl
from jax.experimental.pallas import tpu as pltpu

F32 = jnp.float32
BF16 = jnp.bfloat16

CHUNK = 64
A_HEADS = 4
A_QK_DIM = 64
A_V_DIM = 2 * A_QK_DIM
B_HEADS = 8
B_DIM = 64
A_WIDTH = A_HEADS * A_V_DIM
B_WIDTH = B_HEADS * B_DIM
A_QK_WIDTH = A_HEADS * 2 * A_QK_DIM
ROT_DIM = A_QK_DIM // 4
ROPE_THETA = 500000.0
EPS = 1e-6
LOG2_E = math.log2(math.e)

LANES = 128
BF16_ROWS = 16
MXU_DIM = 256
VMEM_LIMIT = 56 * 1024 * 1024

SLAB = MXU_DIM
PROJ_ROWS = SLAB
KV_TILE = 2 * SLAB
Q_TILE = KV_TILE
STEP_STREAMS = 4
A_STEP_HEADS = STEP_STREAMS // 2
B_STEP_HEADS = STEP_STREAMS
OUT_ROWS = 1024

A_VT_ROWS = A_V_DIM + BF16_ROWS
B_VT_ROWS = B_DIM + BF16_ROWS
B_BIAS_PIECES = 3
N_GROUPS = 8

_NT = (((1,), (1,)), ((), ()))


def _params(n_axes):
    return pltpu.CompilerParams(dimension_semantics=("arbitrary",) * n_axes,
                                vmem_limit_bytes=VMEM_LIMIT)


def _silu(x):
    return x * jax.nn.sigmoid(x)


def _mod_kernel(c_ref, w_ref, b_ref, o_ref):
    a = _silu(c_ref[...]).astype(BF16)
    o_ref[...] = jnp.dot(a, w_ref[...].astype(BF16), preferred_element_type=F32) + b_ref[...]


def _modulation(c_all, w_ada, b_ada):
    rows, d = c_all.shape
    n = w_ada.shape[1]
    bn = 512
    return pl.pallas_call(
        _mod_kernel,
        grid=(n // bn,),
        in_specs=[pl.BlockSpec((rows, d), lambda j: (0, 0)),
                  pl.BlockSpec((d, bn), lambda j: (0, j)),
                  pl.BlockSpec((1, bn), lambda j: (0, j))],
        out_specs=pl.BlockSpec((rows, bn), lambda j: (0, j)),
        out_shape=jax.ShapeDtypeStruct((rows, n), F32),
        compiler_params=_params(1),
        name="modulation",
    )(c_all, w_ada, b_ada.reshape(1, n))


def _group_rms(u, gm):
    sq = u * u
    hi = sq.astype(BF16)
    lo = (sq - hi.astype(F32)).astype(BF16)
    parts = []
    for c in range(u.shape[1] // MXU_DIM):
        sl = slice(MXU_DIM * c, MXU_DIM * (c + 1))
        parts.append(jnp.dot(hi[:, sl], gm, preferred_element_type=F32)
                     + jnp.dot(lo[:, sl], gm, preferred_element_type=F32))
    ms = jnp.concatenate(parts, axis=1)
    return u * lax.rsqrt(ms + EPS)


def _rope_t(xt, cos_t, sin_t):
    half = ROT_DIM // 2
    pieces = []
    for base in range(0, xt.shape[0], A_QK_DIM):
        x1 = xt[base:base + half]
        x2 = xt[base + half:base + ROT_DIM]
        pieces += [x1 * cos_t - x2 * sin_t, x2 * cos_t + x1 * sin_t,
                   xt[base + ROT_DIM:base + A_QK_DIM]]
    return jnp.concatenate(pieces, axis=0)


def _with_ones_rows(xt, rows, n_ones):
    r = lax.broadcasted_iota(jnp.int32, (rows, xt.shape[1]), 0)
    return jnp.concatenate([xt, jnp.where(r < n_ones, 1.0, 0.0)], axis=0)


def _proj_kernel(x_ref, scale_ref, shift_ref, g_ref, w_ref, wz_ref, wf_ref, bf_ref,
                 qna_ref, kna_ref, qnb_ref, knb_ref, cos_ref, sin_ref, gm_ref,
                 ka_o, va_o, za_o, kb_o, vb_o, zb_o, lf_o,
                 qa_o, kab_o, vab_o, qb_o, kbb_o, vbb_o, carry_ref, *, transposed):
    @pl.when(pl.program_id(0) == 0)
    def _():
        carry_ref[...] = jnp.zeros_like(carry_ref)

    tm = x_ref.shape[0]
    x = x_ref[...]
    y = x * lax.rsqrt(jnp.mean(x * x, axis=-1, keepdims=True) + EPS) * g_ref[...]
    h = y * (1.0 + scale_ref[...]) + shift_ref[...]
    hb = h.astype(BF16)
    gm = gm_ref[...]
    cos_t, sin_t = cos_ref[...], sin_ref[...]
    qk_scale = A_QK_DIM ** -0.5 * (LOG2_E if transposed else 1.0)
    lane = lax.broadcasted_iota(jnp.int32, (tm, LANES), 1)

    def group(i):
        w = wz_ref[...] if i == N_GROUPS - 1 else w_ref[:, 512 * i:512 * (i + 1)]
        return jnp.dot(hb, w, preferred_element_type=F32)

    fb = lax.dot_general(wf_ref[...], hb, _NT, preferred_element_type=F32)
    fx = fb + bf_ref[...]
    logf = jnp.minimum(fx, 0.0) - jnp.log1p(jnp.exp(-jnp.abs(fx)))
    lf_o[...] = logf[:B_HEADS]

    qa_t = _rope_t((_group_rms(group(0), gm) * qna_ref[...]).T, cos_t, sin_t) * qk_scale
    ka_t = _rope_t((_group_rms(group(1), gm) * kna_ref[...]).T, cos_t, sin_t)
    ka = ka_t.T
    ka_o[...] = ka_t if transposed else ka
    kab_o[...] = ka.astype(BF16)
    va = group(2)
    for hd in range(A_HEADS):
        va_o[pl.ds(hd, tm, stride=A_HEADS), :] = va[:, A_V_DIM * hd:A_V_DIM * (hd + 1)]
    za_o[...] = group(3)
    qb = _group_rms(group(4), gm) * qnb_ref[...] * qk_scale
    kb = _group_rms(group(5), gm) * knb_ref[...]
    kb_o[...] = kb
    vb = group(6)
    vb_o[...] = vb
    zb_o[...] = group(7)

    if not transposed:
        qa_o[...] = qa_t.T.astype(BF16)
        vab_o[...] = va.astype(BF16)
        qb_o[...] = qb.astype(BF16)
        kbb_o[...] = kb.astype(BF16)
        vbb_o[...] = vb.astype(BF16)
        return

    qa_o[...] = qa_t.astype(BF16)
    va_t = va.T
    for hd in range(A_HEADS):
        blk = _with_ones_rows(va_t[A_V_DIM * hd:A_V_DIM * (hd + 1)], BF16_ROWS, 1)
        vab_o[0, A_VT_ROWS * hd:A_VT_ROWS * (hd + 1), :] = blk.astype(BF16)
    qb_t = qb.T
    vb_t = vb.T
    for hd in range(B_HEADS):
        blk = _with_ones_rows(qb_t[B_DIM * hd:B_DIM * (hd + 1)], LANES - B_DIM, B_BIAS_PIECES)
        qb_o[LANES * hd:LANES * (hd + 1), :] = blk.astype(BF16)
        blk = _with_ones_rows(vb_t[B_DIM * hd:B_DIM * (hd + 1)], BF16_ROWS, 1)
        vbb_o[0, B_VT_ROWS * hd:B_VT_ROWS * (hd + 1), :] = blk.astype(BF16)

    pos = lax.broadcasted_iota(jnp.int32, logf.shape, 1)
    c = logf
    s = 1
    while s < tm:
        c = c + jnp.where(pos >= s, pltpu.roll(c, s, 1), 0.0)
        s *= 2
    c = c + carry_ref[...]
    carry_ref[...] = c[:, tm - 1:tm]

    cum_rows = jnp.concatenate([c, jnp.zeros((LANES - c.shape[0], tm), F32)], axis=0).T
    for hd in range(B_HEADS):
        blk = kb[:, LANES * (hd // 2):LANES * (hd // 2 + 1)]
        if hd % 2:
            blk = pltpu.roll(blk, B_DIM, 1)
        neg = -LOG2_E * jnp.broadcast_to(cum_rows[:, hd:hd + 1], (tm, LANES))
        hi = neg.astype(BF16).astype(F32)
        rest = neg - hi
        mid = rest.astype(BF16).astype(F32)
        lo = rest - mid
        aug = jnp.where(lane < B_DIM, blk,
                        jnp.where(lane == B_DIM, hi,
                                  jnp.where(lane == B_DIM + 1, mid,
                                            jnp.where(lane == B_DIM + 2, lo, 0.0))))
        kbb_o[:, LANES * hd:LANES * (hd + 1)] = aug.astype(BF16)


def _projection(x2, scale, shift, norm_g, w_all, w_zb, wf_t, bf_col, gains, rope_tabs, gm,
                transposed):
    t, d = x2.shape
    tm = PROJ_ROWS
    per_row = scale.shape[0] != 1
    mod_spec = (pl.BlockSpec((tm, d), lambda i: (i, 0)) if per_row
                else pl.BlockSpec((1, d), lambda i: (0, 0)))
    const = lambda shape: pl.BlockSpec(shape, lambda i: (0, 0))
    row = lambda width: pl.BlockSpec((tm, width), lambda i: (i, 0))
    col = lambda height: pl.BlockSpec((height, tm), lambda i: (0, i))
    slab = lambda height: pl.BlockSpec((1, height, tm), lambda i: (i, 0, 0))
    in_specs = [row(d), mod_spec, mod_spec, const((1, d)),
                const(w_all.shape), const(w_zb.shape), const(wf_t.shape), const(bf_col.shape),
                const((1, 512)), const((1, 512)), const((1, 512)), const((1, 512)),
                col(ROT_DIM // 2), col(ROT_DIM // 2), const(gm.shape)]
    f32o = jax.ShapeDtypeStruct((t, 512), F32)
    bf = lambda *shape: jax.ShapeDtypeStruct(shape, BF16)
    va_shape = jax.ShapeDtypeStruct((t * A_HEADS, A_V_DIM), F32)
    va_spec = pl.BlockSpec((tm * A_HEADS, A_V_DIM), lambda i: (i, 0))
    ka_shape = jax.ShapeDtypeStruct((512, t), F32) if transposed else f32o
    ka_spec = col(512) if transposed else row(512)
    out_shape = [ka_shape, va_shape] + [f32o] * 4 + [jax.ShapeDtypeStruct((B_HEADS, t), F32)]
    out_specs = [ka_spec, va_spec] + [row(512)] * 4 + [col(B_HEADS)]
    if transposed:
        out_shape += [bf(512, t), bf(t, 512), bf(t // tm, A_HEADS * A_VT_ROWS, tm),
                      bf(B_HEADS * LANES, t), bf(t, B_HEADS * LANES),
                      bf(t // tm, B_HEADS * B_VT_ROWS, tm)]
        out_specs += [col(512), row(512), slab(A_HEADS * A_VT_ROWS),
                      col(B_HEADS * LANES), row(B_HEADS * LANES), slab(B_HEADS * B_VT_ROWS)]
    else:
        out_shape += [bf(t, 512)] * 6
        out_specs += [row(512)] * 6
    return pl.pallas_call(
        functools.partial(_proj_kernel, transposed=transposed),
        grid=(t // tm,),
        in_specs=in_specs,
        out_specs=out_specs,
        out_shape=out_shape,
        scratch_shapes=[pltpu.VMEM((16, 1), F32)],
        compiler_params=_params(1),
        name="projection_prompt" if transposed else "projection_decode",
    )(x2, scale, shift, norm_g, w_all, w_zb, wf_t, bf_col, *gains, *rope_tabs, gm)


def _col_max(s):
    groups = s.reshape(8, s.shape[0] // 8, s.shape[1])
    return jnp.max(jnp.max(groups, axis=0), axis=0, keepdims=True)


def _kv_sweep(n_full, streams, last_mask, s_sc, mt_sc, m_sc, acc_sc):
    def produce(t, j, slot):
        q_cols, k_of, _ = streams[t]
        s = jnp.dot(k_of(j), q_cols, preferred_element_type=F32)
        s_sc[t, slot] = s
        mt_sc[t, slot] = _col_max(s)

    def consume(t, j, slot, mask):
        v_of = streams[t][2]
        s = s_sc[t, slot]
        if mask is None:
            tile_max = mt_sc[t, slot]
        else:
            s = jnp.where(mask, s, -jnp.inf)
            tile_max = _col_max(s)
        m_prev = m_sc[t]
        m_new = jnp.maximum(m_prev, tile_max)
        p = jnp.exp2(s - m_new).astype(BF16)
        pv = None
        for c in range(KV_TILE // SLAB):
            part = jnp.dot(v_of(j * (KV_TILE // SLAB) + c), p[SLAB * c:SLAB * (c + 1)],
                           preferred_element_type=F32)
            pv = part if pv is None else pv + part
        acc_sc[t] = jnp.exp2(m_prev - m_new) * acc_sc[t] + pv
        m_sc[t] = m_new

    def stage(produced=None, consumed=None, mask=None):
        for t in range(len(streams)):
            if produced is not None:
                produce(t, *produced)
            if consumed is not None:
                consume(t, *consumed, mask)

    m_sc[...] = jnp.full(m_sc.shape, -jnp.inf, F32)
    acc_sc[...] = jnp.zeros_like(acc_sc)
    stage(produced=(0, 0))

    def body(i, carry):
        stage(produced=(2 * i + 1, 1), consumed=(2 * i, 0))
        stage(produced=(2 * i + 2, 0), consumed=(2 * i + 1, 1))
        return carry

    lax.fori_loop(0, n_full // 2, body, 0)

    @pl.when(n_full % 2 == 1)
    def _():
        stage(produced=(n_full, 1), consumed=(n_full - 1, 0))
        stage(consumed=(n_full, 1), mask=last_mask)

    @pl.when(n_full % 2 == 0)
    def _():
        stage(consumed=(n_full, 0), mask=last_mask)


def _sweep_scratch(rows):
    n = STEP_STREAMS
    return [pltpu.VMEM((n, 2, KV_TILE, Q_TILE), F32), pltpu.VMEM((n, 2, 1, Q_TILE), F32),
            pltpu.VMEM((n, 1, Q_TILE), F32), pltpu.VMEM((n, rows, Q_TILE), F32)]


def _resident(block_shape, index_map, single):
    mode = pl.Buffered(1) if single else None
    return pl.BlockSpec(block_shape, index_map, pipeline_mode=mode)


def _key_tile(k_ref, j, lanes):
    return k_ref[pl.ds(pl.multiple_of(j * KV_TILE, KV_TILE), KV_TILE), lanes]


def _lam(lq1_ref, lk1_ref, lq2_ref, lk2_ref, lam_init):
    return (jnp.exp(jnp.sum(lq1_ref[...] * lk1_ref[...], axis=1, keepdims=True))
            - jnp.exp(jnp.sum(lq2_ref[...] * lk2_ref[...], axis=1, keepdims=True)) + lam_init)


def _attn_a_kernel(qt_ref, k_ref, vt_ref, z_ref, lq1_ref, lk1_ref, lq2_ref, lk2_ref, sg_ref,
                   o_ref, s_sc, mt_sc, m_sc, acc_sc, *, lam_init):
    qi = pl.program_id(1)
    row = lax.broadcasted_iota(jnp.int32, (LANES, Q_TILE), 0)
    streams = []
    for hd in range(A_STEP_HEADS):
        lanes = slice(LANES * hd, LANES * (hd + 1))
        rows = slice(A_VT_ROWS * hd, A_VT_ROWS * (hd + 1))
        qt = qt_ref[lanes, :]
        zero = jnp.zeros_like(qt)
        k_of = functools.partial(_key_tile, k_ref, lanes=lanes)
        v_of = lambda sl, rows=rows: vt_ref[sl, rows, :]
        streams.append((jnp.where(row < A_QK_DIM, qt, zero), k_of, v_of))
        streams.append((jnp.where(row >= A_QK_DIM, qt, zero), k_of, v_of))
    shape = (KV_TILE, Q_TILE)
    chunk_mask = (lax.broadcasted_iota(jnp.int32, shape, 0) // CHUNK
                  <= lax.broadcasted_iota(jnp.int32, shape, 1) // CHUNK)
    _kv_sweep(qi, streams, chunk_mask, s_sc, mt_sc, m_sc, acc_sc)

    lam = _lam(lq1_ref, lk1_ref, lq2_ref, lk2_ref, lam_init)
    gain = sg_ref[...] * (1.0 - lam_init)
    for hd in range(A_STEP_HEADS):
        lanes = slice(LANES * hd, LANES * (hd + 1))
        a0, a1 = acc_sc[2 * hd], acc_sc[2 * hd + 1]
        oa = ((a0[:A_V_DIM] / a0[A_V_DIM:A_V_DIM + 1])
              - lam * (a1[:A_V_DIM] / a1[A_V_DIM:A_V_DIM + 1]))
        on = oa * lax.rsqrt(jnp.mean(oa * oa, axis=0, keepdims=True) + EPS) * gain
        o_ref[:, lanes] = (on.T * _silu(z_ref[:, lanes])).astype(o_ref.dtype)


def _attn_a_prompt(qa_t, kab, va_t, za, lam_vecs, subln_col, lam_init):
    t = kab.shape[0]
    nh = A_STEP_HEADS
    vec = pl.BlockSpec((1, A_QK_DIM), lambda g, i: (0, 0))
    return pl.pallas_call(
        functools.partial(_attn_a_kernel, lam_init=lam_init),
        grid=(A_HEADS // nh, t // Q_TILE),
        in_specs=[pl.BlockSpec((nh * LANES, Q_TILE), lambda g, i: (g, i)),
                  _resident((t, nh * LANES), lambda g, i: (0, g), False),
                  _resident((t // SLAB, nh * A_VT_ROWS, SLAB), lambda g, i: (0, g, 0), False),
                  pl.BlockSpec((Q_TILE, nh * LANES), lambda g, i: (i, g)),
                  vec, vec, vec, vec,
                  pl.BlockSpec((A_V_DIM, 1), lambda g, i: (0, 0))],
        out_specs=pl.BlockSpec((Q_TILE, nh * LANES), lambda g, i: (i, g)),
        out_shape=jax.ShapeDtypeStruct((t, A_WIDTH), BF16),
        scratch_shapes=_sweep_scratch(A_VT_ROWS),
        compiler_params=_params(2),
        name="attn_a_prompt",
    )(qa_t, kab, va_t, za, *lam_vecs, subln_col)


def _attn_b_kernel(qt_ref, k_ref, vt_ref, z_ref, o_ref, s_sc, mt_sc, m_sc, acc_sc):
    qi = pl.program_id(1)
    shape = (KV_TILE, Q_TILE)
    causal = lax.broadcasted_iota(jnp.int32, shape, 0) <= lax.broadcasted_iota(jnp.int32, shape, 1)
    streams = []
    for t in range(B_STEP_HEADS):
        lanes = slice(LANES * t, LANES * (t + 1))
        rows = slice(B_VT_ROWS * t, B_VT_ROWS * (t + 1))
        streams.append((qt_ref[lanes, :],
                        functools.partial(_key_tile, k_ref, lanes=lanes),
                        lambda sl, rows=rows: vt_ref[sl, rows, :]))
    _kv_sweep(qi, streams, causal, s_sc, mt_sc, m_sc, acc_sc)
    for p in range(B_STEP_HEADS // 2):
        lanes = slice(LANES * p, LANES * (p + 1))
        a0, a1 = acc_sc[2 * p], acc_sc[2 * p + 1]
        o = jnp.concatenate([a0[:B_DIM] / a0[B_DIM:B_DIM + 1], a1[:B_DIM] / a1[B_DIM:B_DIM + 1]],
                            axis=0)
        o_ref[:, lanes] = (o.T * _silu(z_ref[:, lanes])).astype(o_ref.dtype)


def _attn_b_prompt(qb_t, kb_aug, vb_t, zb):
    t = kb_aug.shape[0]
    nh = B_STEP_HEADS
    out_w = nh * B_DIM
    return pl.pallas_call(
        _attn_b_kernel,
        grid=(B_HEADS // nh, t // Q_TILE),
        in_specs=[pl.BlockSpec((nh * LANES, Q_TILE), lambda g, i: (g, i)),
                  _resident((t, nh * LANES), lambda g, i: (0, g), True),
                  _resident((t // SLAB, nh * B_VT_ROWS, SLAB), lambda g, i: (0, g, 0), True),
                  pl.BlockSpec((Q_TILE, out_w), lambda g, i: (i, g))],
        out_specs=pl.BlockSpec((Q_TILE, out_w), lambda g, i: (i, g)),
        out_shape=jax.ShapeDtypeStruct((t, B_WIDTH), BF16),
        scratch_shapes=_sweep_scratch(B_VT_ROWS),
        compiler_params=_params(2),
        name="attn_b_prompt",
    )(qb_t, kb_aug, vb_t, zb)


def _softmax_two_part(s_c, s_n, pv_c, pv_n):
    m = jnp.maximum(jnp.max(s_c, axis=1, keepdims=True), jnp.max(s_n, axis=1, keepdims=True))
    p_c = jnp.exp(s_c - m)
    p_n = jnp.exp(s_n - m)
    acc = pv_c(p_c.astype(BF16)) + pv_n(p_n.astype(BF16))
    l = jnp.sum(p_c, axis=1, keepdims=True) + jnp.sum(p_n, axis=1, keepdims=True)
    return acc, l


def _diff_epilogue(o0, o1, lam, subln_g, lam_init, z):
    oa = o0 - lam * o1
    on = oa * lax.rsqrt(jnp.mean(oa * oa, axis=-1, keepdims=True) + EPS)
    return on * subln_g * (1.0 - lam_init) * _silu(z)


def _attn_dec_kernel(qa_ref, kan_ref, van_ref, za_ref, qb_ref, kbn_ref, vbn_ref, zb_ref,
                     cak_ref, cav_ref, cbk_ref, cbv_ref, lf_ref,
                     lq1_ref, lk1_ref, lq2_ref, lk2_ref, sg_ref,
                     ga_o, gb_o, *, lam_init):
    tdec = qa_ref.shape[0]
    past = cak_ref.shape[2]
    lane = lax.broadcasted_iota(jnp.int32, (tdec, LANES), 1)
    lam = _lam(lq1_ref, lk1_ref, lq2_ref, lk2_ref, lam_init)
    dot = functools.partial(jnp.dot, preferred_element_type=F32)
    dot_nt = lambda a, b: lax.dot_general(a, b, _NT, preferred_element_type=F32)

    for h in range(A_HEADS):
        cols = slice(LANES * h, LANES * (h + 1))
        q = qa_ref[:, cols]
        zero = jnp.zeros_like(q)
        kt_c = cak_ref[0, cols, :].astype(BF16)
        v_c = cav_ref[0, pl.ds(h, past, stride=A_HEADS), :].astype(BF16)
        k_n = kan_ref[:, cols]
        v_n = van_ref[:, cols]
        outs = []
        for t in range(2):
            qm = jnp.where((lane >= A_QK_DIM) == (t == 1), q, zero)
            acc, l = _softmax_two_part(dot(qm, kt_c), dot_nt(qm, k_n),
                                       lambda p: dot(p, v_c), lambda p: dot(p, v_n))
            outs.append(acc / l)
        ga_o[:, cols] = _diff_epilogue(outs[0], outs[1], lam, sg_ref[...], lam_init,
                                       za_ref[:, cols]).astype(ga_o.dtype)

    lf = lf_ref[0]
    pos = lax.broadcasted_iota(jnp.int32, lf.shape, 1)
    cum = lf
    s = 1
    while s < lf.shape[1]:
        cum = cum + jnp.where(pos >= s, pltpu.roll(cum, s, 1), 0.0)
        s *= 2
    row = lax.broadcasted_iota(jnp.int32, (tdec, tdec), 0)
    col = lax.broadcasted_iota(jnp.int32, (tdec, tdec), 1)
    for p in range(B_HEADS // 2):
        cols = slice(LANES * p, LANES * (p + 1))
        q = qb_ref[:, cols]
        zero = jnp.zeros_like(q)
        kt_c = cbk_ref[0, cols, :].astype(BF16)
        vt_c = cbv_ref[0, cols, :].astype(BF16)
        k_n = kbn_ref[:, cols]
        v_n = vbn_ref[:, cols]
        outs = []
        for t in range(2):
            qm = jnp.where((lane >= B_DIM) == (t == 1), q, zero)
            head = 2 * p + t
            s_c = dot(qm, kt_c) - cum[head:head + 1, :past]
            s_n = dot_nt(qm, k_n) - cum[head:head + 1, past:past + tdec]
            s_n = jnp.where(col <= row, s_n, -jnp.inf)
            acc, l = _softmax_two_part(s_c, s_n, lambda pr: dot_nt(pr, vt_c),
                                       lambda pr: dot(pr, v_n))
            outs.append(acc / l)
        o = jnp.where(lane < B_DIM, outs[0], outs[1])
        gb_o[:, cols] = (o * _silu(zb_ref[:, cols])).astype(gb_o.dtype)


def _attn_decode(proj, caches, lf_cat, lam_vecs, subln_g, lam_init, nb, tdec):
    cak_t, cav, cbk_t, cbv_t = caches
    past = cak_t.shape[2]
    rows = pl.BlockSpec((tdec, 512), lambda b: (b, 0))
    feat_major = pl.BlockSpec((1, 512, past), lambda b: (b, 0, 0))
    vec = pl.BlockSpec((1, A_QK_DIM), lambda b: (0, 0))
    return pl.pallas_call(
        functools.partial(_attn_dec_kernel, lam_init=lam_init),
        grid=(nb,),
        in_specs=[rows] * 8
                 + [feat_major, pl.BlockSpec((1, past * A_HEADS, A_V_DIM), lambda b: (b, 0, 0)),
                    feat_major, feat_major,
                    pl.BlockSpec((1, B_HEADS, lf_cat.shape[2]), lambda b: (b, 0, 0)),
                    vec, vec, vec, vec, pl.BlockSpec((1, A_V_DIM), lambda b: (0, 0))],
        out_specs=[rows, rows],
        out_shape=[jax.ShapeDtypeStruct((nb * tdec, A_WIDTH), BF16),
                   jax.ShapeDtypeStruct((nb * tdec, B_WIDTH), BF16)],
        compiler_params=_params(1),
        name="attn_decode",
    )(*proj, cak_t, cav, cbk_t, cbv_t, lf_cat, *lam_vecs, subln_g)


def _out_kernel(x_ref, gate_ref, ga_ref, gb_ref, w_ref, y_ref):
    o = (jnp.dot(ga_ref[...], w_ref[:A_WIDTH, :], preferred_element_type=F32)
         + jnp.dot(gb_ref[...], w_ref[A_WIDTH:, :], preferred_element_type=F32))
    y_ref[...] = x_ref[...] + gate_ref[...] * o


def _out_projection(x2, gate, ga, gb, w_out):
    t, d = x2.shape
    tm = min(OUT_ROWS, t)
    gate_spec = (pl.BlockSpec((tm, d), lambda i: (i, 0)) if gate.shape[0] != 1
                 else pl.BlockSpec((1, d), lambda i: (0, 0)))
    return pl.pallas_call(
        _out_kernel,
        grid=(t // tm,),
        in_specs=[pl.BlockSpec((tm, d), lambda i: (i, 0)), gate_spec,
                  pl.BlockSpec((tm, A_WIDTH), lambda i: (i, 0)),
                  pl.BlockSpec((tm, B_WIDTH), lambda i: (i, 0)),
                  pl.BlockSpec(w_out.shape, lambda i: (0, 0))],
        out_specs=pl.BlockSpec((tm, d), lambda i: (i, 0)),
        out_shape=jax.ShapeDtypeStruct((t, d), F32),
        compiler_params=_params(1),
        name="out_projection",
    )(x2, gate, ga, gb, w_out)


def _rope_tables(pos):
    inv = ROPE_THETA ** (-jnp.arange(0, ROT_DIM, 2, dtype=F32) / ROT_DIM)
    ang = (pos.astype(F32)[:, None] * inv[None, :]).T
    return jnp.cos(ang), jnp.sin(ang)


def _group_mean_matrix():
    idx = np.arange(MXU_DIM) // A_QK_DIM
    return jnp.asarray((idx[:, None] == idx[None, :]).astype(np.float32) / A_QK_DIM, dtype=BF16)


def kernel(x_prompt, x_sample, cache_a_k, cache_a_v, cache_b_k, cache_b_v, cache_b_logf,
           c_prompt, c_sample, norm_g, w_ada, b_ada, w_in, b_f, qn_a, kn_a,
           lam_q1, lam_k1, lam_q2, lam_k2, subln_g, qn_b, kn_b, w_out):
    depth = norm_g.shape[0]
    bp, seq, d = x_prompt.shape
    bs, tdec, _ = x_sample.shape
    past = cache_a_k.shape[2]
    assert bp == 1 and seq % KV_TILE == 0 and Q_TILE % CHUNK == 0 and seq % OUT_ROWS == 0
    assert past % CHUNK == 0 and tdec == CHUNK and (bs * tdec) % PROJ_ROWS == 0

    xp = x_prompt.reshape(seq, d)
    xs = x_sample.reshape(bs * tdec, d)
    c_rows = bp + bs
    c_all = jnp.concatenate([c_prompt, c_sample, jnp.zeros((16 - c_rows, d), F32)], axis=0)
    rope_p = _rope_tables(jnp.arange(seq))
    rope_s = _rope_tables(jnp.tile(past + jnp.arange(tdec), bs))
    gm = _group_mean_matrix()
    fb_off = 2 * A_QK_WIDTH + 2 * A_WIDTH + 3 * B_WIDTH

    outs_p = [[] for _ in range(5)]
    outs_s = [[] for _ in range(5)]
    for l in range(depth):
        lam_init = 0.8 - 0.6 * math.exp(-0.3 * l)
        w_all = w_in[l].astype(BF16)
        w_zb = w_all[:, fb_off + B_HEADS:]
        wf_t = jnp.concatenate([w_in[l][:, fb_off:fb_off + B_HEADS].T.astype(BF16),
                                jnp.zeros((16 - B_HEADS, d), BF16)], axis=0)
        bf_col = jnp.concatenate([b_f[l], jnp.zeros((16 - B_HEADS,), F32)]).reshape(16, 1)
        gains = tuple(jnp.tile(g[l], 512 // g.shape[1]).reshape(1, 512)
                      for g in (qn_a, kn_a, qn_b, kn_b))
        lam_vecs = tuple(v[l].reshape(1, A_QK_DIM) for v in (lam_q1, lam_k1, lam_q2, lam_k2))
        w_o = w_out[l].astype(BF16)
        g_row = norm_g[l].reshape(1, d)

        mod = _modulation(c_all, w_ada[l], b_ada[l])
        shift, scale, gate = mod[:, :d], mod[:, d:2 * d], mod[:, 2 * d:]

        (ka_t, va, za, kb, vb, zb, lf_t, qa_t, kab, va_t, qb_t, kb_aug, vb_t) = _projection(
            xp, scale[:bp], shift[:bp], g_row, w_all, w_zb, wf_t, bf_col, gains, rope_p, gm, True)
        ga = _attn_a_prompt(qa_t, kab, va_t, za, lam_vecs, subln_g[l].reshape(A_V_DIM, 1), lam_init)
        gb = _attn_b_prompt(qb_t, kb_aug, vb_t, zb)
        xp = _out_projection(xp, gate[:bp], ga, gb, w_o)
        ka = ka_t.reshape(A_HEADS, 2, A_QK_DIM, seq).transpose(3, 0, 1, 2)
        for dst, a in zip(outs_p, (ka.reshape(bp, seq, A_HEADS, 2, A_QK_DIM),
                                   va.reshape(bp, seq, A_HEADS, A_V_DIM),
                                   kb.reshape(bp, seq, B_HEADS, B_DIM),
                                   vb.reshape(bp, seq, B_HEADS, B_DIM),
                                   lf_t.T.reshape(bp, seq, B_HEADS))):
            dst.append(a)

        rep = lambda a: jnp.repeat(a[bp:c_rows], tdec, axis=0)
        (ka, va, za, kb, vb, zb, lf_t, qa, kab, vab, qb, kbb, vbb) = _projection(
            xs, rep(scale), rep(shift), g_row, w_all, w_zb, wf_t, bf_col, gains, rope_s, gm,
            False)
        lf_new = lf_t.reshape(B_HEADS, bs, tdec).transpose(1, 0, 2)
        lf_cat = jnp.concatenate([cache_b_logf[l].astype(F32).transpose(0, 2, 1), lf_new,
                                  jnp.zeros((bs, B_HEADS, LANES - tdec), F32)], axis=2)
        caches = (cache_a_k[l].transpose(0, 2, 3, 4, 1).reshape(bs, A_QK_WIDTH, past),
                  cache_a_v[l].reshape(bs, past * A_HEADS, A_V_DIM),
                  cache_b_k[l].transpose(0, 2, 3, 1).reshape(bs, B_WIDTH, past),
                  cache_b_v[l].transpose(0, 2, 3, 1).reshape(bs, B_WIDTH, past))
        ga, gb = _attn_decode((qa, kab, vab, za, qb, kbb, vbb, zb), caches, lf_cat,
                              lam_vecs, subln_g[l].reshape(1, A_V_DIM), lam_init, bs, tdec)
        xs = _out_projection(xs, rep(gate), ga, gb, w_o)
        for dst, a in zip(outs_s, (ka.reshape(bs, tdec, A_HEADS, 2, A_QK_DIM),
                                   va.reshape(bs, tdec, A_HEADS, A_V_DIM),
                                   kb.reshape(bs, tdec, B_HEADS, B_DIM),
                                   vb.reshape(bs, tdec, B_HEADS, B_DIM),
                                   lf_t.T.reshape(bs, tdec, B_HEADS))):
            dst.append(a)

    return (xp.reshape(bp, seq, d), xs.reshape(bs, tdec, d),
            *(jnp.stack(o) for o in outs_p), *(jnp.stack(o) for o in outs_s))
```

```python
import functools
import math

import jax
import jax.numpy as jnp
from jax import lax
from jax.experimental import pallas as pl
from jax.experimental.pallas import tpu as pltpu

F32 = jnp.float32
BF16 = jnp.bfloat16

CHUNK = 64
A_HEADS = 4
A_QK_DIM = 64
A_V_DIM = 2 * A_QK_DIM
B_HEADS = 8
B_DIM = 64
A_WIDTH = A_HEADS * A_V_DIM
B_WIDTH = B_HEADS * B_DIM
A_QK_WIDTH = A_HEADS * 2 * A_QK_DIM
ROT_DIM = A_QK_DIM // 4
ROPE_THETA = 500000.0
EPS = 1e-6
LOG2_E = math.log2(math.e)

LANES = 128
BF16_ROWS = 16
MXU_DIM = 256
VMEM_LIMIT = 56 * 1024 * 1024

SLAB = MXU_DIM
PROJ_ROWS = SLAB
KV_TILE = 2 * SLAB
Q_TILE = KV_TILE
STEP_STREAMS = 4
A_STEP_HEADS = STEP_STREAMS // 2
B_STEP_HEADS = STEP_STREAMS
OUT_ROWS = 1024

A_VT_ROWS = A_V_DIM + BF16_ROWS
B_VT_ROWS = B_DIM + BF16_ROWS
B_BIAS_PIECES = 3
N_GROUPS = 8

_NT = (((1,), (1,)), ((), ()))


def _params(n_axes):
    return pltpu.CompilerParams(dimension_semantics=("arbitrary",) * n_axes,
                                vmem_limit_bytes=VMEM_LIMIT)


def _silu(x):
    return x * jax.nn.sigmoid(x)


def _mod_kernel(c_ref, w_ref, b_ref, o_ref):
    a = _silu(c_ref[...]).astype(BF16)
    o_ref[...] = jnp.dot(a, w_ref[...].astype(BF16), preferred_element_type=F32) + b_ref[...]


def _modulation(c_all, w_ada, b_ada):
    rows, d = c_all.shape
    n = w_ada.shape[1]
    bn = 512
    return pl.pallas_call(
        _mod_kernel,
        grid=(n // bn,),
        in_specs=[pl.BlockSpec((rows, d), lambda j: (0, 0)),
                  pl.BlockSpec((d, bn), lambda j: (0, j)),
                  pl.BlockSpec((1, bn), lambda j: (0, j))],
        out_specs=pl.BlockSpec((rows, bn), lambda j: (0, j)),
        out_shape=jax.ShapeDtypeStruct((rows, n), F32),
        compiler_params=_params(1),
        name="modulation",
    )(c_all, w_ada, b_ada.reshape(1, n))


def _group_rms_t(u, gain_t):
    ut = u.T
    groups = ut.reshape(ut.shape[0] // A_QK_DIM, A_QK_DIM, ut.shape[1])
    ms = jnp.mean(groups * groups, axis=1, keepdims=True)
    return (groups * lax.rsqrt(ms + EPS)).reshape(ut.shape) * gain_t


def _rope_t(xt, cos_t, sin_t):
    half = ROT_DIM // 2
    pieces = []
    for base in range(0, xt.shape[0], A_QK_DIM):
        x1 = xt[base:base + half]
        x2 = xt[base + half:base + ROT_DIM]
        pieces += [x1 * cos_t - x2 * sin_t, x2 * cos_t + x1 * sin_t,
                   xt[base + ROT_DIM:base + A_QK_DIM]]
    return jnp.concatenate(pieces, axis=0)


def _with_ones_rows(xt, rows, n_ones):
    r = lax.broadcasted_iota(jnp.int32, (rows, xt.shape[1]), 0)
    return jnp.concatenate([xt, jnp.where(r < n_ones, 1.0, 0.0)], axis=0)


def _proj_kernel(x_ref, scale_ref, shift_ref, g_ref, w_ref, wz_ref, wf_ref, bf_ref,
                 qna_ref, kna_ref, qnb_ref, knb_ref, cos_ref, sin_ref,
                 ka_o, va_o, za_o, kb_o, vb_o, zb_o, lf_o,
                 qa_o, kab_o, vab_o, qb_o, kbb_o, vbb_o, carry_ref, *, transposed):
    @pl.when(pl.program_id(0) == 0)
    def _():
        carry_ref[...] = jnp.zeros_like(carry_ref)

    tm = x_ref.shape[0]
    x = x_ref[...]
    y = x * lax.rsqrt(jnp.mean(x * x, axis=-1, keepdims=True) + EPS) * g_ref[...]
    h = y * (1.0 + scale_ref[...]) + shift_ref[...]
    hb = h.astype(BF16)
    cos_t, sin_t = cos_ref[...], sin_ref[...]
    qk_scale = A_QK_DIM ** -0.5 * (LOG2_E if transposed else 1.0)
    lane = lax.broadcasted_iota(jnp.int32, (tm, LANES), 1)

    def group(i):
        w = wz_ref[...] if i == N_GROUPS - 1 else w_ref[:, 512 * i:512 * (i + 1)]
        return jnp.dot(hb, w, preferred_element_type=F32)

    fb = lax.dot_general(wf_ref[...], hb, _NT, preferred_element_type=F32)
    fx = fb + bf_ref[...]
    logf = jnp.minimum(fx, 0.0) - jnp.log1p(jnp.exp(-jnp.abs(fx)))
    lf_o[...] = logf[:B_HEADS]

    qa_t = _rope_t(_group_rms_t(group(0), qna_ref[...]), cos_t, sin_t) * qk_scale
    ka_t = _rope_t(_group_rms_t(group(1), kna_ref[...]), cos_t, sin_t)
    ka = ka_t.T
    ka_o[...] = ka_t if transposed else ka
    kab_o[...] = ka.astype(BF16)
    va = group(2)
    for hd in range(A_HEADS):
        va_o[pl.ds(hd, tm, stride=A_HEADS), :] = va[:, A_V_DIM * hd:A_V_DIM * (hd + 1)]
    za_o[...] = group(3)
    qb_t = _group_rms_t(group(4), qnb_ref[...]) * qk_scale
    kb_t = _group_rms_t(group(5), knb_ref[...])
    kb = kb_t.T
    kb_o[...] = kb_t if transposed else kb
    vb = group(6)
    vb_o[...] = vb
    zb_o[...] = group(7)

    if not transposed:
        qa_o[...] = qa_t.T.astype(BF16)
        vab_o[...] = va.astype(BF16)
        qb_o[...] = qb_t.T.astype(BF16)
        kbb_o[...] = kb.astype(BF16)
        vbb_o[...] = vb.astype(BF16)
        return

    qa_o[...] = qa_t.astype(BF16)
    va_t = va.T
    for hd in range(A_HEADS):
        blk = _with_ones_rows(va_t[A_V_DIM * hd:A_V_DIM * (hd + 1)], BF16_ROWS, 1)
        vab_o[0, A_VT_ROWS * hd:A_VT_ROWS * (hd + 1), :] = blk.astype(BF16)
    vb_t = vb.T
    for hd in range(B_HEADS):
        blk = _with_ones_rows(qb_t[B_DIM * hd:B_DIM * (hd + 1)], LANES - B_DIM, B_BIAS_PIECES)
        qb_o[LANES * hd:LANES * (hd + 1), :] = blk.astype(BF16)
        blk = _with_ones_rows(vb_t[B_DIM * hd:B_DIM * (hd + 1)], BF16_ROWS, 1)
        vbb_o[0, B_VT_ROWS * hd:B_VT_ROWS * (hd + 1), :] = blk.astype(BF16)

    pos = lax.broadcasted_iota(jnp.int32, logf.shape, 1)
    c = logf
    s = 1
    while s < tm:
        c = c + jnp.where(pos >= s, pltpu.roll(c, s, 1), 0.0)
        s *= 2
    c = c + carry_ref[...]
    carry_ref[...] = c[:, tm - 1:tm]

    cum_rows = jnp.concatenate([c, jnp.zeros((LANES - c.shape[0], tm), F32)], axis=0).T
    for hd in range(B_HEADS):
        blk = kb[:, LANES * (hd // 2):LANES * (hd // 2 + 1)]
        if hd % 2:
            blk = pltpu.roll(blk, B_DIM, 1)
        neg = -LOG2_E * jnp.broadcast_to(cum_rows[:, hd:hd + 1], (tm, LANES))
        hi = neg.astype(BF16).astype(F32)
        rest = neg - hi
        mid = rest.astype(BF16).astype(F32)
        lo = rest - mid
        aug = jnp.where(lane < B_DIM, blk,
                        jnp.where(lane == B_DIM, hi,
                                  jnp.where(lane == B_DIM + 1, mid,
                                            jnp.where(lane == B_DIM + 2, lo, 0.0))))
        kbb_o[:, LANES * hd:LANES * (hd + 1)] = aug.astype(BF16)


def _projection(x2, scale, shift, norm_g, w_all, w_zb, wf_t, bf_col, gains, rope_tabs,
                transposed):
    t, d = x2.shape
    tm = PROJ_ROWS
    per_row = scale.shape[0] != 1
    mod_spec = (pl.BlockSpec((tm, d), lambda i: (i, 0)) if per_row
                else pl.BlockSpec((1, d), lambda i: (0, 0)))
    const = lambda shape: pl.BlockSpec(shape, lambda i: (0, 0))
    row = lambda width: pl.BlockSpec((tm, width), lambda i: (i, 0))
    col = lambda height: pl.BlockSpec((height, tm), lambda i: (0, i))
    slab = lambda height: pl.BlockSpec((1, height, tm), lambda i: (i, 0, 0))
    in_specs = [row(d), mod_spec, mod_spec, const((1, d)),
                const(w_all.shape), const(w_zb.shape), const(wf_t.shape), const(bf_col.shape),
                const((512, tm)), const((512, tm)), const((512, tm)), const((512, tm)),
                col(ROT_DIM // 2), col(ROT_DIM // 2)]
    f32o = jax.ShapeDtypeStruct((t, 512), F32)
    bf = lambda *shape: jax.ShapeDtypeStruct(shape, BF16)
    va_shape = jax.ShapeDtypeStruct((t * A_HEADS, A_V_DIM), F32)
    va_spec = pl.BlockSpec((tm * A_HEADS, A_V_DIM), lambda i: (i, 0))
    ka_shape = jax.ShapeDtypeStruct((512, t), F32) if transposed else f32o
    ka_spec = col(512) if transposed else row(512)
    out_shape = ([ka_shape, va_shape, f32o, ka_shape, f32o, f32o]
                 + [jax.ShapeDtypeStruct((B_HEADS, t), F32)])
    out_specs = [ka_spec, va_spec, row(512), ka_spec, row(512), row(512), col(B_HEADS)]
    if transposed:
        out_shape += [bf(512, t), bf(t, 512), bf(t // tm, A_HEADS * A_VT_ROWS, tm),
                      bf(B_HEADS * LANES, t), bf(t, B_HEADS * LANES),
                      bf(t // tm, B_HEADS * B_VT_ROWS, tm)]
        out_specs += [col(512), row(512), slab(A_HEADS * A_VT_ROWS),
                      col(B_HEADS * LANES), row(B_HEADS * LANES), slab(B_HEADS * B_VT_ROWS)]
    else:
        out_shape += [bf(t, 512)] * 6
        out_specs += [row(512)] * 6
    return pl.pallas_call(
        functools.partial(_proj_kernel, transposed=transposed),
        grid=(t // tm,),
        in_specs=in_specs,
        out_specs=out_specs,
        out_shape=out_shape,
        scratch_shapes=[pltpu.VMEM((16, 1), F32)],
        compiler_params=_params(1),
        name="projection_prompt" if transposed else "projection_decode",
    )(x2, scale, shift, norm_g, w_all, w_zb, wf_t, bf_col, *gains, *rope_tabs)


def _col_max(s):
    groups = s.reshape(8, s.shape[0] // 8, s.shape[1])
    return jnp.max(jnp.max(groups, axis=0), axis=0, keepdims=True)


def _kv_sweep(n_full, streams, last_mask, s_sc, mt_sc, m_sc, acc_sc):
    def produce(t, j, slot):
        q_cols, k_of, _ = streams[t]
        s = jnp.dot(k_of(j), q_cols, preferred_element_type=F32)
        s_sc[t, slot] = s
        mt_sc[t, slot] = _col_max(s)

    def consume(t, j, slot, mask):
        v_of = streams[t][2]
        s = s_sc[t, slot]
        if mask is None:
            tile_max = mt_sc[t, slot]
        else:
            s = jnp.where(mask, s, -jnp.inf)
            tile_max = _col_max(s)
        m_prev = m_sc[t]
        m_new = jnp.maximum(m_prev, tile_max)
        p = jnp.exp2(s - m_new).astype(BF16)
        pv = None
        for c in range(KV_TILE // SLAB):
            part = jnp.dot(v_of(j * (KV_TILE // SLAB) + c), p[SLAB * c:SLAB * (c + 1)],
                           preferred_element_type=F32)
            pv = part if pv is None else pv + part
        acc_sc[t] = jnp.exp2(m_prev - m_new) * acc_sc[t] + pv
        m_sc[t] = m_new

    def stage(produced=None, consumed=None, mask=None):
        for t in range(len(streams)):
            if produced is not None:
                produce(t, *produced)
            if consumed is not None:
                consume(t, *consumed, mask)

    m_sc[...] = jnp.full(m_sc.shape, -jnp.inf, F32)
    acc_sc[...] = jnp.zeros_like(acc_sc)
    stage(produced=(0, 0))

    def pairs(first, count):
        for k in range(count):
            j = first + 2 * k
            stage(produced=(j + 1, 1), consumed=(j, 0))
            stage(produced=(j + 2, 0), consumed=(j + 1, 1))

    def body4(i, carry):
        pairs(4 * i, 2)
        return carry

    def body2(i, carry):
        pairs(4 * (n_full // 4) + 2 * i, 1)
        return carry

    lax.fori_loop(0, n_full // 4, body4, 0)
    lax.fori_loop(0, (n_full % 4) // 2, body2, 0)

    @pl.when(n_full % 2 == 1)
    def _():
        stage(produced=(n_full, 1), consumed=(n_full - 1, 0))
        stage(consumed=(n_full, 1), mask=last_mask)

    @pl.when(n_full % 2 == 0)
    def _():
        stage(consumed=(n_full, 0), mask=last_mask)


def _sweep_scratch(rows):
    n = STEP_STREAMS
    return [pltpu.VMEM((n, 2, KV_TILE, Q_TILE), F32), pltpu.VMEM((n, 2, 1, Q_TILE), F32),
            pltpu.VMEM((n, 1, Q_TILE), F32), pltpu.VMEM((n, rows, Q_TILE), F32)]


def _resident(block_shape, index_map, single):
    mode = pl.Buffered(1) if single else None
    return pl.BlockSpec(block_shape, index_map, pipeline_mode=mode)


def _key_tile(k_ref, j, lanes):
    return k_ref[pl.ds(pl.multiple_of(j * KV_TILE, KV_TILE), KV_TILE), lanes]


def _lam(lq1_ref, lk1_ref, lq2_ref, lk2_ref, lam_init):
    return (jnp.exp(jnp.sum(lq1_ref[...] * lk1_ref[...], axis=1, keepdims=True))
            - jnp.exp(jnp.sum(lq2_ref[...] * lk2_ref[...], axis=1, keepdims=True)) + lam_init)


def _attn_a_kernel(qt_ref, k_ref, vt_ref, z_ref, lq1_ref, lk1_ref, lq2_ref, lk2_ref, sg_ref,
                   o_ref, s_sc, mt_sc, m_sc, acc_sc, *, lam_init):
    qi = pl.program_id(1)
    row = lax.broadcasted_iota(jnp.int32, (LANES, Q_TILE), 0)
    streams = []
    for hd in range(A_STEP_HEADS):
        lanes = slice(LANES * hd, LANES * (hd + 1))
        rows = slice(A_VT_ROWS * hd, A_VT_ROWS * (hd + 1))
        qt = qt_ref[lanes, :]
        zero = jnp.zeros_like(qt)
        k_of = functools.partial(_key_tile, k_ref, lanes=lanes)
        v_of = lambda sl, rows=rows: vt_ref[sl, rows, :]
        streams.append((jnp.where(row < A_QK_DIM, qt, zero), k_of, v_of))
        streams.append((jnp.where(row >= A_QK_DIM, qt, zero), k_of, v_of))
    shape = (KV_TILE, Q_TILE)
    chunk_mask = (lax.broadcasted_iota(jnp.int32, shape, 0) // CHUNK
                  <= lax.broadcasted_iota(jnp.int32, shape, 1) // CHUNK)
    _kv_sweep(qi, streams, chunk_mask, s_sc, mt_sc, m_sc, acc_sc)

    lam = _lam(lq1_ref, lk1_ref, lq2_ref, lk2_ref, lam_init)
    gain = sg_ref[...] * (1.0 - lam_init)
    for hd in range(A_STEP_HEADS):
        lanes = slice(LANES * hd, LANES * (hd + 1))
        a0, a1 = acc_sc[2 * hd], acc_sc[2 * hd + 1]
        oa = ((a0[:A_V_DIM] / a0[A_V_DIM:A_V_DIM + 1])
              - lam * (a1[:A_V_DIM] / a1[A_V_DIM:A_V_DIM + 1]))
        on = oa * lax.rsqrt(jnp.mean(oa * oa, axis=0, keepdims=True) + EPS) * gain
        o_ref[:, lanes] = (on.T * _silu(z_ref[:, lanes])).astype(o_ref.dtype)


def _attn_a_prompt(qa_t, kab, va_t, za, lam_vecs, subln_col, lam_init):
    t = kab.shape[0]
    nh = A_STEP_HEADS
    vec = pl.BlockSpec((1, A_QK_DIM), lambda g, i: (0, 0))
    return pl.pallas_call(
        functools.partial(_attn_a_kernel, lam_init=lam_init),
        grid=(A_HEADS // nh, t // Q_TILE),
        in_specs=[pl.BlockSpec((nh * LANES, Q_TILE), lambda g, i: (g, i)),
                  _resident((t, nh * LANES), lambda g, i: (0, g), False),
                  _resident((t // SLAB, nh * A_VT_ROWS, SLAB), lambda g, i: (0, g, 0), False),
                  pl.BlockSpec((Q_TILE, nh * LANES), lambda g, i: (i, g)),
                  vec, vec, vec, vec,
                  pl.BlockSpec((A_V_DIM, 1), lambda g, i: (0, 0))],
        out_specs=pl.BlockSpec((Q_TILE, nh * LANES), lambda g, i: (i, g)),
        out_shape=jax.ShapeDtypeStruct((t, A_WIDTH), BF16),
        scratch_shapes=_sweep_scratch(A_VT_ROWS),
        compiler_params=_params(2),
        name="attn_a_prompt",
    )(qa_t, kab, va_t, za, *lam_vecs, subln_col)


def _attn_b_kernel(qt_ref, k_ref, vt_ref, z_ref, o_ref, s_sc, mt_sc, m_sc, acc_sc):
    qi = pl.program_id(1)
    shape = (KV_TILE, Q_TILE)
    causal = lax.broadcasted_iota(jnp.int32, shape, 0) <= lax.broadcasted_iota(jnp.int32, shape, 1)
    streams = []
    for t in range(B_STEP_HEADS):
        lanes = slice(LANES * t, LANES * (t + 1))
        rows = slice(B_VT_ROWS * t, B_VT_ROWS * (t + 1))
        streams.append((qt_ref[lanes, :],
                        functools.partial(_key_tile, k_ref, lanes=lanes),
                        lambda sl, rows=rows: vt_ref[sl, rows, :]))
    _kv_sweep(qi, streams, causal, s_sc, mt_sc, m_sc, acc_sc)
    for p in range(B_STEP_HEADS // 2):
        lanes = slice(LANES * p, LANES * (p + 1))
        a0, a1 = acc_sc[2 * p], acc_sc[2 * p + 1]
        o = jnp.concatenate([a0[:B_DIM] / a0[B_DIM:B_DIM + 1], a1[:B_DIM] / a1[B_DIM:B_DIM + 1]],
                            axis=0)
        o_ref[:, lanes] = (o.T * _silu(z_ref[:, lanes])).astype(o_ref.dtype)


def _attn_b_prompt(qb_t, kb_aug, vb_t, zb):
    t = kb_aug.shape[0]
    nh = B_STEP_HEADS
    out_w = nh * B_DIM
    return pl.pallas_call(
        _attn_b_kernel,
        grid=(B_HEADS // nh, t // Q_TILE),
        in_specs=[pl.BlockSpec((nh * LANES, Q_TILE), lambda g, i: (g, i)),
                  _resident((t, nh * LANES), lambda g, i: (0, g), True),
                  _resident((t // SLAB, nh * B_VT_ROWS, SLAB), lambda g, i: (0, g, 0), True),
                  pl.BlockSpec((Q_TILE, out_w), lambda g, i: (i, g))],
        out_specs=pl.BlockSpec((Q_TILE, out_w), lambda g, i: (i, g)),
        out_shape=jax.ShapeDtypeStruct((t, B_WIDTH), BF16),
        scratch_shapes=_sweep_scratch(B_VT_ROWS),
        compiler_params=_params(2),
        name="attn_b_prompt",
    )(qb_t, kb_aug, vb_t, zb)


def _softmax_two_part(s_c, s_n, pv_c, pv_n):
    m = jnp.maximum(jnp.max(s_c, axis=1, keepdims=True), jnp.max(s_n, axis=1, keepdims=True))
    p_c = jnp.exp(s_c - m)
    p_n = jnp.exp(s_n - m)
    acc = pv_c(p_c.astype(BF16)) + pv_n(p_n.astype(BF16))
    l = jnp.sum(p_c, axis=1, keepdims=True) + jnp.sum(p_n, axis=1, keepdims=True)
    return acc, l


def _diff_epilogue(o0, o1, lam, subln_g, lam_init, z):
    oa = o0 - lam * o1
    on = oa * lax.rsqrt(jnp.mean(oa * oa, axis=-1, keepdims=True) + EPS)
    return on * subln_g * (1.0 - lam_init) * _silu(z)


def _attn_dec_kernel(qa_ref, kan_ref, van_ref, za_ref, qb_ref, kbn_ref, vbn_ref, zb_ref,
                     cak_ref, cav_ref, cbk_ref, cbv_ref, lf_ref,
                     lq1_ref, lk1_ref, lq2_ref, lk2_ref, sg_ref,
                     ga_o, gb_o, *, lam_init):
    tdec = qa_ref.shape[0]
    past = cak_ref.shape[2]
    lane = lax.broadcasted_iota(jnp.int32, (tdec, LANES), 1)
    lam = _lam(lq1_ref, lk1_ref, lq2_ref, lk2_ref, lam_init)
    dot = functools.partial(jnp.dot, preferred_element_type=F32)
    dot_nt = lambda a, b: lax.dot_general(a, b, _NT, preferred_element_type=F32)

    for h in range(A_HEADS):
        cols = slice(LANES * h, LANES * (h + 1))
        q = qa_ref[:, cols]
        zero = jnp.zeros_like(q)
        kt_c = cak_ref[0, cols, :].astype(BF16)
        v_c = cav_ref[0, pl.ds(h, past, stride=A_HEADS), :].astype(BF16)
        k_n = kan_ref[:, cols]
        v_n = van_ref[:, cols]
        outs = []
        for t in range(2):
            qm = jnp.where((lane >= A_QK_DIM) == (t == 1), q, zero)
            acc, l = _softmax_two_part(dot(qm, kt_c), dot_nt(qm, k_n),
                                       lambda p: dot(p, v_c), lambda p: dot(p, v_n))
            outs.append(acc / l)
        ga_o[:, cols] = _diff_epilogue(outs[0], outs[1], lam, sg_ref[...], lam_init,
                                       za_ref[:, cols]).astype(ga_o.dtype)

    lf = lf_ref[0]
    pos = lax.broadcasted_iota(jnp.int32, lf.shape, 1)
    cum = lf
    s = 1
    while s < lf.shape[1]:
        cum = cum + jnp.where(pos >= s, pltpu.roll(cum, s, 1), 0.0)
        s *= 2
    row = lax.broadcasted_iota(jnp.int32, (tdec, tdec), 0)
    col = lax.broadcasted_iota(jnp.int32, (tdec, tdec), 1)
    for p in range(B_HEADS // 2):
        cols = slice(LANES * p, LANES * (p + 1))
        q = qb_ref[:, cols]
        zero = jnp.zeros_like(q)
        kt_c = cbk_ref[0, cols, :].astype(BF16)
        vt_c = cbv_ref[0, cols, :].astype(BF16)
        k_n = kbn_ref[:, cols]
        v_n = vbn_ref[:, cols]
        outs = []
        for t in range(2):
            qm = jnp.where((lane >= B_DIM) == (t == 1), q, zero)
            head = 2 * p + t
            s_c = dot(qm, kt_c) - cum[head:head + 1, :past]
            s_n = dot_nt(qm, k_n) - cum[head:head + 1, past:past + tdec]
            s_n = jnp.where(col <= row, s_n, -jnp.inf)
            acc, l = _softmax_two_part(s_c, s_n, lambda pr: dot_nt(pr, vt_c),
                                       lambda pr: dot(pr, v_n))
            outs.append(acc / l)
        o = jnp.where(lane < B_DIM, outs[0], outs[1])
        gb_o[:, cols] = (o * _silu(zb_ref[:, cols])).astype(gb_o.dtype)


def _attn_decode(proj, caches, lf_cat, lam_vecs, subln_g, lam_init, nb, tdec):
    cak_t, cav, cbk_t, cbv_t = caches
    past = cak_t.shape[2]
    rows = pl.BlockSpec((tdec, 512), lambda b: (b, 0))
    feat_major = pl.BlockSpec((1, 512, past), lambda b: (b, 0, 0))
    vec = pl.BlockSpec((1, A_QK_DIM), lambda b: (0, 0))
    return pl.pallas_call(
        functools.partial(_attn_dec_kernel, lam_init=lam_init),
        grid=(nb,),
        in_specs=[rows] * 8
                 + [feat_major, pl.BlockSpec((1, past * A_HEADS, A_V_DIM), lambda b: (b, 0, 0)),
                    feat_major, feat_major,
                    pl.BlockSpec((1, B_HEADS, lf_cat.shape[2]), lambda b: (b, 0, 0)),
                    vec, vec, vec, vec, pl.BlockSpec((1, A_V_DIM), lambda b: (0, 0))],
        out_specs=[rows, rows],
        out_shape=[jax.ShapeDtypeStruct((nb * tdec, A_WIDTH), BF16),
                   jax.ShapeDtypeStruct((nb * tdec, B_WIDTH), BF16)],
        compiler_params=_params(1),
        name="attn_decode",
    )(*proj, cak_t, cav, cbk_t, cbv_t, lf_cat, *lam_vecs, subln_g)


def _out_kernel(x_ref, gate_ref, ga_ref, gb_ref, w_ref, y_ref):
    o = (jnp.dot(ga_ref[...], w_ref[:A_WIDTH, :], preferred_element_type=F32)
         + jnp.dot(gb_ref[...], w_ref[A_WIDTH:, :], preferred_element_type=F32))
    y_ref[...] = x_ref[...] + gate_ref[...] * o


def _out_projection(x2, gate, ga, gb, w_out):
    t, d = x2.shape
    tm = min(OUT_ROWS, t)
    gate_spec = (pl.BlockSpec((tm, d), lambda i: (i, 0)) if gate.shape[0] != 1
                 else pl.BlockSpec((1, d), lambda i: (0, 0)))
    return pl.pallas_call(
        _out_kernel,
        grid=(t // tm,),
        in_specs=[pl.BlockSpec((tm, d), lambda i: (i, 0)), gate_spec,
                  pl.BlockSpec((tm, A_WIDTH), lambda i: (i, 0)),
                  pl.BlockSpec((tm, B_WIDTH), lambda i: (i, 0)),
                  pl.BlockSpec(w_out.shape, lambda i: (0, 0))],
        out_specs=pl.BlockSpec((tm, d), lambda i: (i, 0)),
        out_shape=jax.ShapeDtypeStruct((t, d), F32),
        compiler_params=_params(1),
        name="out_projection",
    )(x2, gate, ga, gb, w_out)


def _rope_tables(pos):
    inv = ROPE_THETA ** (-jnp.arange(0, ROT_DIM, 2, dtype=F32) / ROT_DIM)
    ang = (pos.astype(F32)[:, None] * inv[None, :]).T
    return jnp.cos(ang), jnp.sin(ang)


def kernel(x_prompt, x_sample, cache_a_k, cache_a_v, cache_b_k, cache_b_v, cache_b_logf,
           c_prompt, c_sample, norm_g, w_ada, b_ada, w_in, b_f, qn_a, kn_a,
           lam_q1, lam_k1, lam_q2, lam_k2, subln_g, qn_b, kn_b, w_out):
    depth = norm_g.shape[0]
    bp, seq, d = x_prompt.shape
    bs, tdec, _ = x_sample.shape
    past = cache_a_k.shape[2]
    assert bp == 1 and seq % KV_TILE == 0 and Q_TILE % CHUNK == 0 and seq % OUT_ROWS == 0
    assert past % CHUNK == 0 and tdec == CHUNK and (bs * tdec) % PROJ_ROWS == 0

    xp = x_prompt.reshape(seq, d)
    xs = x_sample.reshape(bs * tdec, d)
    c_rows = bp + bs
    c_all = jnp.concatenate([c_prompt, c_sample, jnp.zeros((16 - c_rows, d), F32)], axis=0)
    rope_p = _rope_tables(jnp.arange(seq))
    rope_s = _rope_tables(jnp.tile(past + jnp.arange(tdec), bs))
    fb_off = 2 * A_QK_WIDTH + 2 * A_WIDTH + 3 * B_WIDTH

    outs_p = [[] for _ in range(5)]
    outs_s = [[] for _ in range(5)]
    for l in range(depth):
        lam_init = 0.8 - 0.6 * math.exp(-0.3 * l)
        w_all = w_in[l].astype(BF16)
        w_zb = w_all[:, fb_off + B_HEADS:]
        wf_t = jnp.concatenate([w_in[l][:, fb_off:fb_off + B_HEADS].T.astype(BF16),
                                jnp.zeros((16 - B_HEADS, d), BF16)], axis=0)
        bf_col = jnp.concatenate([b_f[l], jnp.zeros((16 - B_HEADS,), F32)]).reshape(16, 1)
        gains = tuple(jnp.broadcast_to(jnp.tile(g[l], 512 // g.shape[1])[:, None],
                                       (512, PROJ_ROWS))
                      for g in (qn_a, kn_a, qn_b, kn_b))
        lam_vecs = tuple(v[l].reshape(1, A_QK_DIM) for v in (lam_q1, lam_k1, lam_q2, lam_k2))
        w_o = w_out[l].astype(BF16)
        g_row = norm_g[l].reshape(1, d)

        mod = _modulation(c_all, w_ada[l], b_ada[l])
        shift, scale, gate = mod[:, :d], mod[:, d:2 * d], mod[:, 2 * d:]

        (ka_t, va, za, kb_t, vb, zb, lf_t, qa_t, kab, va_t, qb_t, kb_aug, vb_t) = _projection(
            xp, scale[:bp], shift[:bp], g_row, w_all, w_zb, wf_t, bf_col, gains, rope_p, True)
        ga = _attn_a_prompt(qa_t, kab, va_t, za, lam_vecs, subln_g[l].reshape(A_V_DIM, 1), lam_init)
        gb = _attn_b_prompt(qb_t, kb_aug, vb_t, zb)
        xp = _out_projection(xp, gate[:bp], ga, gb, w_o)
        ka = ka_t.reshape(A_HEADS, 2, A_QK_DIM, seq).transpose(3, 0, 1, 2)
        kb = kb_t.reshape(B_HEADS, B_DIM, seq).transpose(2, 0, 1)
        for dst, a in zip(outs_p, (ka.reshape(bp, seq, A_HEADS, 2, A_QK_DIM),
                                   va.reshape(bp, seq, A_HEADS, A_V_DIM),
                                   kb.reshape(bp, seq, B_HEADS, B_DIM),
                                   vb.reshape(bp, seq, B_HEADS, B_DIM),
                                   lf_t.T.reshape(bp, seq, B_HEADS))):
            dst.append(a)

        rep = lambda a: jnp.repeat(a[bp:c_rows], tdec, axis=0)
        (ka, va, za, kb, vb, zb, lf_t, qa, kab, vab, qb, kbb, vbb) = _projection(
            xs, rep(scale), rep(shift), g_row, w_all, w_zb, wf_t, bf_col, gains, rope_s, False)
        lf_new = lf_t.reshape(B_HEADS, bs, tdec).transpose(1, 0, 2)
        lf_cat = jnp.concatenate([cache_b_logf[l].astype(F32).transpose(0, 2, 1), lf_new,
                                  jnp.zeros((bs, B_HEADS, LANES - tdec), F32)], axis=2)
        caches = (cache_a_k[l].transpose(0, 2, 3, 4, 1).reshape(bs, A_QK_WIDTH, past),
                  cache_a_v[l].reshape(bs, past * A_HEADS, A_V_DIM),
                  cache_b_k[l].transpose(0, 2, 3, 1).reshape(bs, B_WIDTH, past),
                  cache_b_v[l].transpose(0, 2, 3, 1).reshape(bs, B_WIDTH, past))
        ga, gb = _attn_decode((qa, kab, vab, za, qb, kbb, vbb, zb), caches, lf_cat,
                              lam_vecs, subln_g[l].reshape(1, A_V_DIM), lam_init, bs, tdec)
        xs = _out_projection(xs, rep(gate), ga, gb, w_o)
        for dst, a in zip(outs_s, (ka.reshape(bs, tdec, A_HEADS, 2, A_QK_DIM),
                                   va.reshape(bs, tdec, A_HEADS, A_V_DIM),
                                   kb.reshape(bs, tdec, B_HEADS, B_DIM),
                                   vb.reshape(bs, tdec, B_HEADS, B_DIM),
                                   lf_t.T.reshape(bs, tdec, B_HEADS))):
            dst.append(a)

    return (xp.reshape(bp, seq, d), xs.reshape(bs, tdec, d),
            *(jnp.stack(o) for o in outs_p), *(jnp.stack(o) for o in outs_s))
```

```python
import functools
import math

import jax
import jax.numpy as jnp
from jax import lax
from jax.experimental import pallas as pl
from jax.experimental.pallas import tpu as pltpu

F32 = jnp.float32
BF16 = jnp.bfloat16

CHUNK = 64
A_HEADS = 4
A_QK_DIM = 64
A_V_DIM = 2 * A_QK_DIM
B_HEADS = 8
B_DIM = 64
A_WIDTH = A_HEADS * A_V_DIM
B_WIDTH = B_HEADS * B_DIM
A_QK_WIDTH = A_HEADS * 2 * A_QK_DIM
ROT_DIM = A_QK_DIM // 4
ROPE_THETA = 500000.0
EPS = 1e-6
LOG2_E = math.log2(math.e)

LANES = 128
BF16_ROWS = 16
MXU_DIM = 256
VMEM_LIMIT = 56 * 1024 * 1024

SLAB = MXU_DIM
PROJ_ROWS = SLAB
KV_TILE = 2 * SLAB
Q_TILE = KV_TILE
STEP_STREAMS = 4
A_STEP_HEADS = STEP_STREAMS // 2
B_STEP_HEADS = STEP_STREAMS
OUT_ROWS = 1024

A_VT_ROWS = A_V_DIM + BF16_ROWS
B_VT_ROWS = B_DIM + BF16_ROWS
B_BIAS_PIECES = 3
N_GROUPS = 8

_NT = (((1,), (1,)), ((), ()))


def _params(n_axes):
    return pltpu.CompilerParams(dimension_semantics=("arbitrary",) * n_axes,
                                vmem_limit_bytes=VMEM_LIMIT)


def _silu(x):
    return x * jax.nn.sigmoid(x)


def _mod_kernel(c_ref, w_ref, b_ref, o_ref):
    a = _silu(c_ref[...]).astype(BF16)
    o_ref[...] = jnp.dot(a, w_ref[...].astype(BF16), preferred_element_type=F32) + b_ref[...]


def _modulation(c_all, w_ada, b_ada):
    rows, d = c_all.shape
    n = w_ada.shape[1]
    bn = 512
    return pl.pallas_call(
        _mod_kernel,
        grid=(n // bn,),
        in_specs=[pl.BlockSpec((rows, d), lambda j: (0, 0)),
                  pl.BlockSpec((d, bn), lambda j: (0, j)),
                  pl.BlockSpec((1, bn), lambda j: (0, j))],
        out_specs=pl.BlockSpec((rows, bn), lambda j: (0, j)),
        out_shape=jax.ShapeDtypeStruct((rows, n), F32),
        compiler_params=_params(1),
        name="modulation",
    )(c_all, w_ada, b_ada.reshape(1, n))


def _group_rms_t(u, gain_t):
    ut = u.T
    groups = ut.reshape(ut.shape[0] // A_QK_DIM, A_QK_DIM, ut.shape[1])
    ms = jnp.mean(groups * groups, axis=1, keepdims=True)
    return (groups * lax.rsqrt(ms + EPS)).reshape(ut.shape) * gain_t


def _rope_t(xt, cos_t, sin_t):
    half = ROT_DIM // 2
    pieces = []
    for base in range(0, xt.shape[0], A_QK_DIM):
        x1 = xt[base:base + half]
        x2 = xt[base + half:base + ROT_DIM]
        pieces += [x1 * cos_t - x2 * sin_t, x2 * cos_t + x1 * sin_t,
                   xt[base + ROT_DIM:base + A_QK_DIM]]
    return jnp.concatenate(pieces, axis=0)


def _with_ones_rows(xt, rows, n_ones):
    r = lax.broadcasted_iota(jnp.int32, (rows, xt.shape[1]), 0)
    return jnp.concatenate([xt, jnp.where(r < n_ones, 1.0, 0.0)], axis=0)


def _proj_kernel(x_ref, scale_ref, shift_ref, g_ref, w_ref, wz_ref, wf_ref, bf_ref,
                 qna_ref, kna_ref, qnb_ref, knb_ref, cos_ref, sin_ref,
                 ka_o, va_o, za_o, kb_o, vb_o, zb_o, lf_o,
                 qa_o, kab_o, vab_o, qb_o, kbb_o, vbb_o, carry_ref, *, transposed):
    @pl.when(pl.program_id(0) == 0)
    def _():
        carry_ref[...] = jnp.zeros_like(carry_ref)

    tm = x_ref.shape[0]
    x = x_ref[...]
    y = x * lax.rsqrt(jnp.mean(x * x, axis=-1, keepdims=True) + EPS) * g_ref[...]
    h = y * (1.0 + scale_ref[...]) + shift_ref[...]
    hb = h.astype(BF16)
    cos_t, sin_t = cos_ref[...], sin_ref[...]
    qk_scale = A_QK_DIM ** -0.5 * (LOG2_E if transposed else 1.0)
    lane = lax.broadcasted_iota(jnp.int32, (tm, LANES), 1)

    def group(i):
        w = wz_ref[...] if i == N_GROUPS - 1 else w_ref[:, 512 * i:512 * (i + 1)]
        return jnp.dot(hb, w, preferred_element_type=F32)

    fb = lax.dot_general(wf_ref[...], hb, _NT, preferred_element_type=F32)
    fx = fb + bf_ref[...]
    logf = jnp.minimum(fx, 0.0) - jnp.log1p(jnp.exp(-jnp.abs(fx)))
    lf_o[...] = logf[:B_HEADS]

    qa_t = _rope_t(_group_rms_t(group(0), qna_ref[...]), cos_t, sin_t) * qk_scale
    ka_t = _rope_t(_group_rms_t(group(1), kna_ref[...]), cos_t, sin_t)
    ka = ka_t.T
    ka_o[...] = ka_t if transposed else ka
    kab_o[...] = ka.astype(BF16)
    va = group(2)
    for hd in range(A_HEADS):
        va_o[pl.ds(hd, tm, stride=A_HEADS), :] = va[:, A_V_DIM * hd:A_V_DIM * (hd + 1)]
    za_o[...] = group(3).astype(BF16)
    qb_t = _group_rms_t(group(4), qnb_ref[...]) * qk_scale
    kb_t = _group_rms_t(group(5), knb_ref[...])
    kb = kb_t.T
    kb_o[...] = kb_t if transposed else kb
    vb = group(6)
    vb_o[...] = vb
    zb_o[...] = group(7).astype(BF16)

    if not transposed:
        qa_o[...] = qa_t.T.astype(BF16)
        vab_o[...] = va.astype(BF16)
        qb_o[...] = qb_t.T.astype(BF16)
        kbb_o[...] = kb.astype(BF16)
        vbb_o[...] = vb.astype(BF16)
        return

    qa_o[...] = qa_t.astype(BF16)
    va_t = va.T
    for hd in range(A_HEADS):
        blk = _with_ones_rows(va_t[A_V_DIM * hd:A_V_DIM * (hd + 1)], BF16_ROWS, 1)
        vab_o[0, A_VT_ROWS * hd:A_VT_ROWS * (hd + 1), :] = blk.astype(BF16)
    vb_t = vb.T
    for hd in range(B_HEADS):
        blk = _with_ones_rows(qb_t[B_DIM * hd:B_DIM * (hd + 1)], LANES - B_DIM, B_BIAS_PIECES)
        qb_o[LANES * hd:LANES * (hd + 1), :] = blk.astype(BF16)
        blk = _with_ones_rows(vb_t[B_DIM * hd:B_DIM * (hd + 1)], BF16_ROWS, 1)
        vbb_o[0, B_VT_ROWS * hd:B_VT_ROWS * (hd + 1), :] = blk.astype(BF16)

    pos = lax.broadcasted_iota(jnp.int32, logf.shape, 1)
    c = logf
    s = 1
    while s < tm:
        c = c + jnp.where(pos >= s, pltpu.roll(c, s, 1), 0.0)
        s *= 2
    c = c + carry_ref[...]
    carry_ref[...] = c[:, tm - 1:tm]

    cum_rows = jnp.concatenate([c, jnp.zeros((LANES - c.shape[0], tm), F32)], axis=0).T
    for hd in range(B_HEADS):
        blk = kb[:, LANES * (hd // 2):LANES * (hd // 2 + 1)]
        if hd % 2:
            blk = pltpu.roll(blk, B_DIM, 1)
        neg = -LOG2_E * jnp.broadcast_to(cum_rows[:, hd:hd + 1], (tm, LANES))
        hi = neg.astype(BF16).astype(F32)
        rest = neg - hi
        mid = rest.astype(BF16).astype(F32)
        lo = rest - mid
        aug = jnp.where(lane < B_DIM, blk,
                        jnp.where(lane == B_DIM, hi,
                                  jnp.where(lane == B_DIM + 1, mid,
                                            jnp.where(lane == B_DIM + 2, lo, 0.0))))
        kbb_o[:, LANES * hd:LANES * (hd + 1)] = aug.astype(BF16)


def _projection(x2, scale, shift, norm_g, w_all, w_zb, wf_t, bf_col, gains, rope_tabs,
                transposed):
    t, d = x2.shape
    tm = PROJ_ROWS
    per_row = scale.shape[0] != 1
    mod_spec = (pl.BlockSpec((tm, d), lambda i: (i, 0)) if per_row
                else pl.BlockSpec((1, d), lambda i: (0, 0)))
    const = lambda shape: pl.BlockSpec(shape, lambda i: (0, 0))
    row = lambda width: pl.BlockSpec((tm, width), lambda i: (i, 0))
    col = lambda height: pl.BlockSpec((height, tm), lambda i: (0, i))
    slab = lambda height: pl.BlockSpec((1, height, tm), lambda i: (i, 0, 0))
    in_specs = [row(d), mod_spec, mod_spec, const((1, d)),
                const(w_all.shape), const(w_zb.shape), const(wf_t.shape), const(bf_col.shape),
                const((512, tm)), const((512, tm)), const((512, tm)), const((512, tm)),
                col(ROT_DIM // 2), col(ROT_DIM // 2)]
    f32o = jax.ShapeDtypeStruct((t, 512), F32)
    bf = lambda *shape: jax.ShapeDtypeStruct(shape, BF16)
    va_shape = jax.ShapeDtypeStruct((t * A_HEADS, A_V_DIM), F32)
    va_spec = pl.BlockSpec((tm * A_HEADS, A_V_DIM), lambda i: (i, 0))
    ka_shape = jax.ShapeDtypeStruct((512, t), F32) if transposed else f32o
    ka_spec = col(512) if transposed else row(512)
    gate = jax.ShapeDtypeStruct((t, 512), BF16)
    out_shape = ([ka_shape, va_shape, gate, ka_shape, f32o, gate]
                 + [jax.ShapeDtypeStruct((B_HEADS, t), F32)])
    out_specs = [ka_spec, va_spec, row(512), ka_spec, row(512), row(512), col(B_HEADS)]
    if transposed:
        out_shape += [bf(512, t), bf(t, 512), bf(t // tm, A_HEADS * A_VT_ROWS, tm),
                      bf(B_HEADS * LANES, t), bf(t, B_HEADS * LANES),
                      bf(t // tm, B_HEADS * B_VT_ROWS, tm)]
        out_specs += [col(512), row(512), slab(A_HEADS * A_VT_ROWS),
                      col(B_HEADS * LANES), row(B_HEADS * LANES), slab(B_HEADS * B_VT_ROWS)]
    else:
        out_shape += [bf(t, 512)] * 6
        out_specs += [row(512)] * 6
    return pl.pallas_call(
        functools.partial(_proj_kernel, transposed=transposed),
        grid=(t // tm,),
        in_specs=in_specs,
        out_specs=out_specs,
        out_shape=out_shape,
        scratch_shapes=[pltpu.VMEM((16, 1), F32)],
        compiler_params=_params(1),
        name="projection_prompt" if transposed else "projection_decode",
    )(x2, scale, shift, norm_g, w_all, w_zb, wf_t, bf_col, *gains, *rope_tabs)


def _col_max(s):
    groups = s.reshape(8, s.shape[0] // 8, s.shape[1])
    return jnp.max(jnp.max(groups, axis=0), axis=0, keepdims=True)


def _kv_sweep(n_full, streams, last_mask, s_sc, mt_sc, m_sc, acc_sc):
    def produce(t, j, slot):
        q_cols, k_of, _ = streams[t]
        s = jnp.dot(k_of(j), q_cols, preferred_element_type=F32)
        s_sc[t, slot] = s
        mt_sc[t, slot] = _col_max(s)

    def consume(t, j, slot, mask):
        v_of = streams[t][2]
        s = s_sc[t, slot]
        if mask is None:
            tile_max = mt_sc[t, slot]
        else:
            s = jnp.where(mask, s, -jnp.inf)
            tile_max = _col_max(s)
        m_prev = m_sc[t]
        m_new = jnp.maximum(m_prev, tile_max)
        p = jnp.exp2(s - m_new).astype(BF16)
        pv = None
        for c in range(KV_TILE // SLAB):
            part = jnp.dot(v_of(j * (KV_TILE // SLAB) + c), p[SLAB * c:SLAB * (c + 1)],
                           preferred_element_type=F32)
            pv = part if pv is None else pv + part
        acc_sc[t] = jnp.exp2(m_prev - m_new) * acc_sc[t] + pv
        m_sc[t] = m_new

    def stage(produced=None, consumed=None, mask=None):
        for t in range(len(streams)):
            if produced is not None:
                produce(t, *produced)
            if consumed is not None:
                consume(t, *consumed, mask)

    m_sc[...] = jnp.full(m_sc.shape, -jnp.inf, F32)
    acc_sc[...] = jnp.zeros_like(acc_sc)
    stage(produced=(0, 0))

    def pairs(first, count):
        for k in range(count):
            j = first + 2 * k
            stage(produced=(j + 1, 1), consumed=(j, 0))
            stage(produced=(j + 2, 0), consumed=(j + 1, 1))

    def body4(i, carry):
        pairs(4 * i, 2)
        return carry

    def body2(i, carry):
        pairs(4 * (n_full // 4) + 2 * i, 1)
        return carry

    lax.fori_loop(0, n_full // 4, body4, 0)
    lax.fori_loop(0, (n_full % 4) // 2, body2, 0)

    @pl.when(n_full % 2 == 1)
    def _():
        stage(produced=(n_full, 1), consumed=(n_full - 1, 0))
        stage(consumed=(n_full, 1), mask=last_mask)

    @pl.when(n_full % 2 == 0)
    def _():
        stage(consumed=(n_full, 0), mask=last_mask)


def _sweep_scratch(rows):
    n = STEP_STREAMS
    return [pltpu.VMEM((n, 2, KV_TILE, Q_TILE), F32), pltpu.VMEM((n, 2, 1, Q_TILE), F32),
            pltpu.VMEM((n, 1, Q_TILE), F32), pltpu.VMEM((n, rows, Q_TILE), F32)]


def _resident(block_shape, index_map, single):
    mode = pl.Buffered(1) if single else None
    return pl.BlockSpec(block_shape, index_map, pipeline_mode=mode)


def _key_tile(k_ref, j, lanes):
    return k_ref[pl.ds(pl.multiple_of(j * KV_TILE, KV_TILE), KV_TILE), lanes]


def _lam(lq1_ref, lk1_ref, lq2_ref, lk2_ref, lam_init):
    return (jnp.exp(jnp.sum(lq1_ref[...] * lk1_ref[...], axis=1, keepdims=True))
            - jnp.exp(jnp.sum(lq2_ref[...] * lk2_ref[...], axis=1, keepdims=True)) + lam_init)


def _attn_a_kernel(qt_ref, k_ref, vt_ref, z_ref, lq1_ref, lk1_ref, lq2_ref, lk2_ref, sg_ref,
                   o_ref, s_sc, mt_sc, m_sc, acc_sc, *, lam_init):
    qi = pl.program_id(1)
    row = lax.broadcasted_iota(jnp.int32, (LANES, Q_TILE), 0)
    streams = []
    for hd in range(A_STEP_HEADS):
        lanes = slice(LANES * hd, LANES * (hd + 1))
        rows = slice(A_VT_ROWS * hd, A_VT_ROWS * (hd + 1))
        qt = qt_ref[lanes, :]
        zero = jnp.zeros_like(qt)
        k_of = functools.partial(_key_tile, k_ref, lanes=lanes)
        v_of = lambda sl, rows=rows: vt_ref[sl, rows, :]
        streams.append((jnp.where(row < A_QK_DIM, qt, zero), k_of, v_of))
        streams.append((jnp.where(row >= A_QK_DIM, qt, zero), k_of, v_of))
    shape = (KV_TILE, Q_TILE)
    chunk_mask = (lax.broadcasted_iota(jnp.int32, shape, 0) // CHUNK
                  <= lax.broadcasted_iota(jnp.int32, shape, 1) // CHUNK)
    _kv_sweep(qi, streams, chunk_mask, s_sc, mt_sc, m_sc, acc_sc)

    lam = _lam(lq1_ref, lk1_ref, lq2_ref, lk2_ref, lam_init)
    gain = sg_ref[...] * (1.0 - lam_init)
    for hd in range(A_STEP_HEADS):
        lanes = slice(LANES * hd, LANES * (hd + 1))
        a0, a1 = acc_sc[2 * hd], acc_sc[2 * hd + 1]
        oa = ((a0[:A_V_DIM] / a0[A_V_DIM:A_V_DIM + 1])
              - lam * (a1[:A_V_DIM] / a1[A_V_DIM:A_V_DIM + 1]))
        on = oa * lax.rsqrt(jnp.mean(oa * oa, axis=0, keepdims=True) + EPS) * gain
        o_ref[:, lanes] = (on.T * _silu(z_ref[:, lanes].astype(F32))).astype(o_ref.dtype)


def _attn_a_prompt(qa_t, kab, va_t, za, lam_vecs, subln_col, lam_init):
    t = kab.shape[0]
    nh = A_STEP_HEADS
    vec = pl.BlockSpec((1, A_QK_DIM), lambda g, i: (0, 0))
    return pl.pallas_call(
        functools.partial(_attn_a_kernel, lam_init=lam_init),
        grid=(A_HEADS // nh, t // Q_TILE),
        in_specs=[pl.BlockSpec((nh * LANES, Q_TILE), lambda g, i: (g, i)),
                  _resident((t, nh * LANES), lambda g, i: (0, g), False),
                  _resident((t // SLAB, nh * A_VT_ROWS, SLAB), lambda g, i: (0, g, 0), False),
                  pl.BlockSpec((Q_TILE, nh * LANES), lambda g, i: (i, g)),
                  vec, vec, vec, vec,
                  pl.BlockSpec((A_V_DIM, 1), lambda g, i: (0, 0))],
        out_specs=pl.BlockSpec((Q_TILE, nh * LANES), lambda g, i: (i, g)),
        out_shape=jax.ShapeDtypeStruct((t, A_WIDTH), BF16),
        scratch_shapes=_sweep_scratch(A_VT_ROWS),
        compiler_params=_params(2),
        name="attn_a_prompt",
    )(qa_t, kab, va_t, za, *lam_vecs, subln_col)


def _attn_b_kernel(qt_ref, k_ref, vt_ref, z_ref, o_ref, s_sc, mt_sc, m_sc, acc_sc):
    qi = pl.program_id(1)
    shape = (KV_TILE, Q_TILE)
    causal = lax.broadcasted_iota(jnp.int32, shape, 0) <= lax.broadcasted_iota(jnp.int32, shape, 1)
    streams = []
    for t in range(B_STEP_HEADS):
        lanes = slice(LANES * t, LANES * (t + 1))
        rows = slice(B_VT_ROWS * t, B_VT_ROWS * (t + 1))
        streams.append((qt_ref[lanes, :],
                        functools.partial(_key_tile, k_ref, lanes=lanes),
                        lambda sl, rows=rows: vt_ref[sl, rows, :]))
    _kv_sweep(qi, streams, causal, s_sc, mt_sc, m_sc, acc_sc)
    for p in range(B_STEP_HEADS // 2):
        lanes = slice(LANES * p, LANES * (p + 1))
        a0, a1 = acc_sc[2 * p], acc_sc[2 * p + 1]
        o = jnp.concatenate([a0[:B_DIM] / a0[B_DIM:B_DIM + 1], a1[:B_DIM] / a1[B_DIM:B_DIM + 1]],
                            axis=0)
        o_ref[:, lanes] = (o.T * _silu(z_ref[:, lanes].astype(F32))).astype(o_ref.dtype)


def _attn_b_prompt(qb_t, kb_aug, vb_t, zb):
    t = kb_aug.shape[0]
    nh = B_STEP_HEADS
    out_w = nh * B_DIM
    return pl.pallas_call(
        _attn_b_kernel,
        grid=(B_HEADS // nh, t // Q_TILE),
        in_specs=[pl.BlockSpec((nh * LANES, Q_TILE), lambda g, i: (g, i)),
                  _resident((t, nh * LANES), lambda g, i: (0, g), True),
                  _resident((t // SLAB, nh * B_VT_ROWS, SLAB), lambda g, i: (0, g, 0), True),
                  pl.BlockSpec((Q_TILE, out_w), lambda g, i: (i, g))],
        out_specs=pl.BlockSpec((Q_TILE, out_w), lambda g, i: (i, g)),
        out_shape=jax.ShapeDtypeStruct((t, B_WIDTH), BF16),
        scratch_shapes=_sweep_scratch(B_VT_ROWS),
        compiler_params=_params(2),
        name="attn_b_prompt",
    )(qb_t, kb_aug, vb_t, zb)


def _softmax_streams(scores, values):
    ms = [jnp.maximum(jnp.max(s_c, axis=1, keepdims=True), jnp.max(s_n, axis=1, keepdims=True))
          for s_c, s_n in scores]
    ps = [(jnp.exp(s_c - m), jnp.exp(s_n - m)) for (s_c, s_n), m in zip(scores, ms)]
    ls = [jnp.sum(p_c, axis=1, keepdims=True) + jnp.sum(p_n, axis=1, keepdims=True)
          for p_c, p_n in ps]
    accs = [pv_c(p_c.astype(BF16)) + pv_n(p_n.astype(BF16))
            for (p_c, p_n), (pv_c, pv_n) in zip(ps, values)]
    return [acc / l for acc, l in zip(accs, ls)]


def _diff_epilogue(o0, o1, lam, subln_g, lam_init, z):
    oa = o0 - lam * o1
    on = oa * lax.rsqrt(jnp.mean(oa * oa, axis=-1, keepdims=True) + EPS)
    return on * subln_g * (1.0 - lam_init) * _silu(z)


def _attn_dec_kernel(qa_ref, kan_ref, van_ref, za_ref, qb_ref, kbn_ref, vbn_ref, zb_ref,
                     cak_ref, cav_ref, cbk_ref, cbv_ref, lf_ref,
                     lq1_ref, lk1_ref, lq2_ref, lk2_ref, sg_ref,
                     ga_o, gb_o, *, lam_init):
    tdec = qa_ref.shape[0]
    past = cak_ref.shape[2]
    lane = lax.broadcasted_iota(jnp.int32, (tdec, LANES), 1)
    lam = _lam(lq1_ref, lk1_ref, lq2_ref, lk2_ref, lam_init)
    dot = functools.partial(jnp.dot, preferred_element_type=F32)
    dot_nt = lambda a, b: lax.dot_general(a, b, _NT, preferred_element_type=F32)
    scores, values = [], []

    for h in range(A_HEADS):
        cols = slice(LANES * h, LANES * (h + 1))
        q = qa_ref[:, cols]
        zero = jnp.zeros_like(q)
        kt_c = cak_ref[0, cols, :].astype(BF16)
        v_c = cav_ref[0, pl.ds(h, past, stride=A_HEADS), :].astype(BF16)
        k_n = kan_ref[:, cols]
        v_n = van_ref[:, cols]
        for t in range(2):
            qm = jnp.where((lane >= A_QK_DIM) == (t == 1), q, zero)
            scores.append((dot(qm, kt_c), dot_nt(qm, k_n)))
            values.append((lambda p, v=v_c: dot(p, v), lambda p, v=v_n: dot(p, v)))

    lf = lf_ref[0]
    pos = lax.broadcasted_iota(jnp.int32, lf.shape, 1)
    cum = lf
    s = 1
    while s < lf.shape[1]:
        cum = cum + jnp.where(pos >= s, pltpu.roll(cum, s, 1), 0.0)
        s *= 2
    row = lax.broadcasted_iota(jnp.int32, (tdec, tdec), 0)
    col = lax.broadcasted_iota(jnp.int32, (tdec, tdec), 1)
    for p in range(B_HEADS // 2):
        cols = slice(LANES * p, LANES * (p + 1))
        q = qb_ref[:, cols]
        zero = jnp.zeros_like(q)
        kt_c = cbk_ref[0, cols, :].astype(BF16)
        vt_c = cbv_ref[0, cols, :].astype(BF16)
        k_n = kbn_ref[:, cols]
        v_n = vbn_ref[:, cols]
        for t in range(2):
            qm = jnp.where((lane >= B_DIM) == (t == 1), q, zero)
            head = 2 * p + t
            s_c = dot(qm, kt_c) - cum[head:head + 1, :past]
            s_n = dot_nt(qm, k_n) - cum[head:head + 1, past:past + tdec]
            scores.append((s_c, jnp.where(col <= row, s_n, -jnp.inf)))
            values.append((lambda pr, v=vt_c: dot_nt(pr, v), lambda pr, v=v_n: dot(pr, v)))

    outs = _softmax_streams(scores, values)
    for h in range(A_HEADS):
        cols = slice(LANES * h, LANES * (h + 1))
        ga_o[:, cols] = _diff_epilogue(outs[2 * h], outs[2 * h + 1], lam, sg_ref[...], lam_init,
                                       za_ref[:, cols].astype(F32)).astype(ga_o.dtype)
    outs_b = outs[2 * A_HEADS:]
    for p in range(B_HEADS // 2):
        cols = slice(LANES * p, LANES * (p + 1))
        o = jnp.where(lane < B_DIM, outs_b[2 * p], outs_b[2 * p + 1])
        gb_o[:, cols] = (o * _silu(zb_ref[:, cols].astype(F32))).astype(gb_o.dtype)


def _attn_decode(proj, caches, lf_cat, lam_vecs, subln_g, lam_init, nb, tdec):
    cak_t, cav, cbk_t, cbv_t = caches
    past = cak_t.shape[2]
    rows = pl.BlockSpec((tdec, 512), lambda b: (b, 0))
    feat_major = pl.BlockSpec((1, 512, past), lambda b: (b, 0, 0))
    vec = pl.BlockSpec((1, A_QK_DIM), lambda b: (0, 0))
    return pl.pallas_call(
        functools.partial(_attn_dec_kernel, lam_init=lam_init),
        grid=(nb,),
        in_specs=[rows] * 8
                 + [feat_major, pl.BlockSpec((1, past * A_HEADS, A_V_DIM), lambda b: (b, 0, 0)),
                    feat_major, feat_major,
                    pl.BlockSpec((1, B_HEADS, lf_cat.shape[2]), lambda b: (b, 0, 0)),
                    vec, vec, vec, vec, pl.BlockSpec((1, A_V_DIM), lambda b: (0, 0))],
        out_specs=[rows, rows],
        out_shape=[jax.ShapeDtypeStruct((nb * tdec, A_WIDTH), BF16),
                   jax.ShapeDtypeStruct((nb * tdec, B_WIDTH), BF16)],
        compiler_params=_params(1),
        name="attn_decode",
    )(*proj, cak_t, cav, cbk_t, cbv_t, lf_cat, *lam_vecs, subln_g)


def _out_kernel(x_ref, gate_ref, ga_ref, gb_ref, w_ref, y_ref):
    o = (jnp.dot(ga_ref[...], w_ref[:A_WIDTH, :], preferred_element_type=F32)
         + jnp.dot(gb_ref[...], w_ref[A_WIDTH:, :], preferred_element_type=F32))
    y_ref[...] = x_ref[...] + gate_ref[...] * o


def _out_projection(x2, gate, ga, gb, w_out):
    t, d = x2.shape
    tm = min(OUT_ROWS, t)
    gate_spec = (pl.BlockSpec((tm, d), lambda i: (i, 0)) if gate.shape[0] != 1
                 else pl.BlockSpec((1, d), lambda i: (0, 0)))
    return pl.pallas_call(
        _out_kernel,
        grid=(t // tm,),
        in_specs=[pl.BlockSpec((tm, d), lambda i: (i, 0)), gate_spec,
                  pl.BlockSpec((tm, A_WIDTH), lambda i: (i, 0)),
                  pl.BlockSpec((tm, B_WIDTH), lambda i: (i, 0)),
                  pl.BlockSpec(w_out.shape, lambda i: (0, 0))],
        out_specs=pl.BlockSpec((tm, d), lambda i: (i, 0)),
        out_shape=jax.ShapeDtypeStruct((t, d), F32),
        compiler_params=_params(1),
        name="out_projection",
    )(x2, gate, ga, gb, w_out)


def _rope_tables(pos):
    inv = ROPE_THETA ** (-jnp.arange(0, ROT_DIM, 2, dtype=F32) / ROT_DIM)
    ang = (pos.astype(F32)[:, None] * inv[None, :]).T
    return jnp.cos(ang), jnp.sin(ang)


def kernel(x_prompt, x_sample, cache_a_k, cache_a_v, cache_b_k, cache_b_v, cache_b_logf,
           c_prompt, c_sample, norm_g, w_ada, b_ada, w_in, b_f, qn_a, kn_a,
           lam_q1, lam_k1, lam_q2, lam_k2, subln_g, qn_b, kn_b, w_out):
    depth = norm_g.shape[0]
    bp, seq, d = x_prompt.shape
    bs, tdec, _ = x_sample.shape
    past = cache_a_k.shape[2]
    assert bp == 1 and seq % KV_TILE == 0 and Q_TILE % CHUNK == 0 and seq % OUT_ROWS == 0
    assert past % CHUNK == 0 and tdec == CHUNK and (bs * tdec) % PROJ_ROWS == 0

    xp = x_prompt.reshape(seq, d)
    xs = x_sample.reshape(bs * tdec, d)
    c_rows = bp + bs
    c_all = jnp.concatenate([c_prompt, c_sample, jnp.zeros((16 - c_rows, d), F32)], axis=0)
    rope_p = _rope_tables(jnp.arange(seq))
    rope_s = _rope_tables(jnp.tile(past + jnp.arange(tdec), bs))
    fb_off = 2 * A_QK_WIDTH + 2 * A_WIDTH + 3 * B_WIDTH

    outs_p = [[] for _ in range(5)]
    outs_s = [[] for _ in range(5)]
    for l in range(depth):
        lam_init = 0.8 - 0.6 * math.exp(-0.3 * l)
        w_all = w_in[l].astype(BF16)
        w_zb = w_all[:, fb_off + B_HEADS:]
        wf_t = jnp.concatenate([w_in[l][:, fb_off:fb_off + B_HEADS].T.astype(BF16),
                                jnp.zeros((16 - B_HEADS, d), BF16)], axis=0)
        bf_col = jnp.concatenate([b_f[l], jnp.zeros((16 - B_HEADS,), F32)]).reshape(16, 1)
        gains = tuple(jnp.broadcast_to(jnp.tile(g[l], 512 // g.shape[1])[:, None],
                                       (512, PROJ_ROWS))
                      for g in (qn_a, kn_a, qn_b, kn_b))
        lam_vecs = tuple(v[l].reshape(1, A_QK_DIM) for v in (lam_q1, lam_k1, lam_q2, lam_k2))
        w_o = w_out[l].astype(BF16)
        g_row = norm_g[l].reshape(1, d)

        mod = _modulation(c_all, w_ada[l], b_ada[l])
        shift, scale, gate = mod[:, :d], mod[:, d:2 * d], mod[:, 2 * d:]

        (ka_t, va, za, kb_t, vb, zb, lf_t, qa_t, kab, va_t, qb_t, kb_aug, vb_t) = _projection(
            xp, scale[:bp], shift[:bp], g_row, w_all, w_zb, wf_t, bf_col, gains, rope_p, True)
        ga = _attn_a_prompt(qa_t, kab, va_t, za, lam_vecs, subln_g[l].reshape(A_V_DIM, 1), lam_init)
        gb = _attn_b_prompt(qb_t, kb_aug, vb_t, zb)
        xp = _out_projection(xp, gate[:bp], ga, gb, w_o)
        ka = ka_t.reshape(A_HEADS, 2, A_QK_DIM, seq).transpose(3, 0, 1, 2)
        kb = kb_t.reshape(B_HEADS, B_DIM, seq).transpose(2, 0, 1)
        for dst, a in zip(outs_p, (ka.reshape(bp, seq, A_HEADS, 2, A_QK_DIM),
                                   va.reshape(bp, seq, A_HEADS, A_V_DIM),
                                   kb.reshape(bp, seq, B_HEADS, B_DIM),
                                   vb.reshape(bp, seq, B_HEADS, B_DIM),
                                   lf_t.T.reshape(bp, seq, B_HEADS))):
            dst.append(a)

        rep = lambda a: jnp.repeat(a[bp:c_rows], tdec, axis=0)
        (ka, va, za, kb, vb, zb, lf_t, qa, kab, vab, qb, kbb, vbb) = _projection(
            xs, rep(scale), rep(shift), g_row, w_all, w_zb, wf_t, bf_col, gains, rope_s, False)
        lf_new = lf_t.reshape(B_HEADS, bs, tdec).transpose(1, 0, 2)
        lf_cat = jnp.concatenate([cache_b_logf[l].astype(F32).transpose(0, 2, 1), lf_new,
                                  jnp.zeros((bs, B_HEADS, LANES - tdec), F32)], axis=2)
        caches = (cache_a_k[l].transpose(0, 2, 3, 4, 1).reshape(bs, A_QK_WIDTH, past),
                  cache_a_v[l].reshape(bs, past * A_HEADS, A_V_DIM),
                  cache_b_k[l].transpose(0, 2, 3, 1).reshape(bs, B_WIDTH, past),
                  cache_b_v[l].transpose(0, 2, 3, 1).reshape(bs, B_WIDTH, past))
        ga, gb = _attn_decode((qa, kab, vab, za, qb, kbb, vbb, zb), caches, lf_cat,
                              lam_vecs, subln_g[l].reshape(1, A_V_DIM), lam_init, bs, tdec)
        xs = _out_projection(xs, rep(gate), ga, gb, w_o)
        for dst, a in zip(outs_s, (ka.reshape(bs, tdec, A_HEADS, 2, A_QK_DIM),
                                   va.reshape(bs, tdec, A_HEADS, A_V_DIM),
                                   kb.reshape(bs, tdec, B_HEADS, B_DIM),
                                   vb.reshape(bs, tdec, B_HEADS, B_DIM),
                                   lf_t.T.reshape(bs, tdec, B_HEADS))):
            dst.append(a)

    return (xp.reshape(bp, seq, d), xs.reshape(bs, tdec, d),
            *(jnp.stack(o) for o in outs_p), *(jnp.stack(o) for o in outs_s))
```

```python
import functools
import math

import jax
import jax.numpy as jnp
from jax import lax
from jax.experimental import pallas as pl
from jax.experimental.pallas import tpu as pltpu

F32 = jnp.float32
BF16 = jnp.bfloat16

CHUNK = 64
A_HEADS = 4
A_QK_DIM = 64
A_V_DIM = 2 * A_QK_DIM
B_HEADS = 8
B_DIM = 64
A_WIDTH = A_HEADS * A_V_DIM
B_WIDTH = B_HEADS * B_DIM
A_QK_WIDTH = A_HEADS * 2 * A_QK_DIM
ROT_DIM = A_QK_DIM // 4
ROPE_THETA = 500000.0
EPS = 1e-6
LOG2_E = math.log2(math.e)

LANES = 128
BF16_ROWS = 16
MXU_DIM = 256
VMEM_LIMIT = 56 * 1024 * 1024

SLAB = MXU_DIM
PROJ_ROWS = SLAB
KV_TILE = 2 * SLAB
Q_TILE = KV_TILE
STEP_STREAMS = 4
A_STEP_HEADS = STEP_STREAMS // 2
B_STEP_HEADS = STEP_STREAMS
OUT_ROWS = 1024

A_VT_ROWS = A_V_DIM + BF16_ROWS
B_VT_ROWS = B_DIM + BF16_ROWS
B_BIAS_PIECES = 3
N_GROUPS = 8
GROUP_W = A_QK_WIDTH
assert GROUP_W == A_WIDTH == B_WIDTH

_NT = (((1,), (1,)), ((), ()))


def _params(n_axes):
    return pltpu.CompilerParams(dimension_semantics=("arbitrary",) * n_axes,
                                vmem_limit_bytes=VMEM_LIMIT)


def _silu(x):
    return x * jax.nn.sigmoid(x)


def _mod_kernel(c_ref, w_ref, b_ref, o_ref):
    a = _silu(c_ref[...]).astype(BF16)
    o_ref[...] = jnp.dot(a, w_ref[...].astype(BF16), preferred_element_type=F32) + b_ref[...]


def _modulation(c_all, w_ada, b_ada):
    rows, d = c_all.shape
    n = w_ada.shape[1]
    bn = GROUP_W
    return pl.pallas_call(
        _mod_kernel,
        grid=(n // bn,),
        in_specs=[pl.BlockSpec((rows, d), lambda j: (0, 0)),
                  pl.BlockSpec((d, bn), lambda j: (0, j)),
                  pl.BlockSpec((1, bn), lambda j: (0, j))],
        out_specs=pl.BlockSpec((rows, bn), lambda j: (0, j)),
        out_shape=jax.ShapeDtypeStruct((rows, n), F32),
        compiler_params=_params(1),
        name="modulation",
    )(c_all, w_ada, b_ada.reshape(1, n))


def _group_rms_t(u, gain_t):
    ut = u.T
    groups = ut.reshape(ut.shape[0] // A_QK_DIM, A_QK_DIM, ut.shape[1])
    ms = jnp.mean(groups * groups, axis=1, keepdims=True)
    return (groups * lax.rsqrt(ms + EPS)).reshape(ut.shape) * gain_t


def _rope_t(xt, cos_t, sin_t):
    half = ROT_DIM // 2
    pieces = []
    for base in range(0, xt.shape[0], A_QK_DIM):
        x1 = xt[base:base + half]
        x2 = xt[base + half:base + ROT_DIM]
        pieces += [x1 * cos_t - x2 * sin_t, x2 * cos_t + x1 * sin_t,
                   xt[base + ROT_DIM:base + A_QK_DIM]]
    return jnp.concatenate(pieces, axis=0)


def _with_ones_rows(xt, rows, n_ones):
    r = lax.broadcasted_iota(jnp.int32, (rows, xt.shape[1]), 0)
    return jnp.concatenate([xt, jnp.where(r < n_ones, 1.0, 0.0)], axis=0)


def _proj_kernel(x_ref, scale_ref, shift_ref, g_ref, w_ref, wz_ref, wf_ref, bf_ref,
                 qna_ref, kna_ref, qnb_ref, knb_ref, cos_ref, sin_ref,
                 ka_o, va_o, za_o, kb_o, vb_o, zb_o, lf_o,
                 qa_o, kab_o, vab_o, qb_o, kbb_o, vbb_o, carry_ref, *, transposed):
    @pl.when(pl.program_id(0) == 0)
    def _():
        carry_ref[...] = jnp.zeros_like(carry_ref)

    tm = x_ref.shape[0]
    x = x_ref[...]
    y = x * lax.rsqrt(jnp.mean(x * x, axis=-1, keepdims=True) + EPS) * g_ref[...]
    h = y * (1.0 + scale_ref[...]) + shift_ref[...]
    hb = h.astype(BF16)
    cos_t, sin_t = cos_ref[...], sin_ref[...]
    qk_scale = A_QK_DIM ** -0.5 * (LOG2_E if transposed else 1.0)
    lane = lax.broadcasted_iota(jnp.int32, (tm, LANES), 1)

    def group(i):
        w = wz_ref[...] if i == N_GROUPS - 1 else w_ref[:, GROUP_W * i:GROUP_W * (i + 1)]
        return jnp.dot(hb, w, preferred_element_type=F32)

    fb = lax.dot_general(wf_ref[...], hb, _NT, preferred_element_type=F32)
    fx = fb + bf_ref[...]
    logf = jnp.minimum(fx, 0.0) - jnp.log1p(jnp.exp(-jnp.abs(fx)))
    lf_o[...] = logf[:B_HEADS]

    qa_t = _rope_t(_group_rms_t(group(0), qna_ref[...]), cos_t, sin_t) * qk_scale
    ka_t = _rope_t(_group_rms_t(group(1), kna_ref[...]), cos_t, sin_t)
    ka = ka_t.T
    ka_o[...] = ka_t if transposed else ka
    kab_o[...] = ka.astype(BF16)
    va = group(2)
    for hd in range(A_HEADS):
        va_o[pl.ds(hd, tm, stride=A_HEADS), :] = va[:, A_V_DIM * hd:A_V_DIM * (hd + 1)]
    za_o[...] = group(3).astype(BF16)
    qb_t = _group_rms_t(group(4), qnb_ref[...]) * qk_scale
    kb_t = _group_rms_t(group(5), knb_ref[...])
    kb = kb_t.T
    kb_o[...] = kb_t if transposed else kb
    vb = group(6)
    vb_o[...] = vb
    zb_o[...] = group(7).astype(BF16)

    if not transposed:
        qa_o[...] = qa_t.T.astype(BF16)
        vab_o[...] = va.astype(BF16)
        qb_o[...] = qb_t.T.astype(BF16)
        kbb_o[...] = kb.astype(BF16)
        vbb_o[...] = vb.astype(BF16)
        return

    qa_o[...] = qa_t.astype(BF16)
    va_t = va.T
    for hd in range(A_HEADS):
        blk = _with_ones_rows(va_t[A_V_DIM * hd:A_V_DIM * (hd + 1)], BF16_ROWS, 1)
        vab_o[0, A_VT_ROWS * hd:A_VT_ROWS * (hd + 1), :] = blk.astype(BF16)
    vb_t = vb.T
    for hd in range(B_HEADS):
        blk = _with_ones_rows(qb_t[B_DIM * hd:B_DIM * (hd + 1)], LANES - B_DIM, B_BIAS_PIECES)
        qb_o[LANES * hd:LANES * (hd + 1), :] = blk.astype(BF16)
        blk = _with_ones_rows(vb_t[B_DIM * hd:B_DIM * (hd + 1)], BF16_ROWS, 1)
        vbb_o[0, B_VT_ROWS * hd:B_VT_ROWS * (hd + 1), :] = blk.astype(BF16)

    pos = lax.broadcasted_iota(jnp.int32, logf.shape, 1)
    c = logf
    s = 1
    while s < tm:
        c = c + jnp.where(pos >= s, pltpu.roll(c, s, 1), 0.0)
        s *= 2
    c = c + carry_ref[...]
    carry_ref[...] = c[:, tm - 1:tm]

    cum_rows = jnp.concatenate([c, jnp.zeros((LANES - c.shape[0], tm), F32)], axis=0).T
    for hd in range(B_HEADS):
        blk = kb[:, LANES * (hd // 2):LANES * (hd // 2 + 1)]
        if hd % 2:
            blk = pltpu.roll(blk, B_DIM, 1)
        neg = -LOG2_E * jnp.broadcast_to(cum_rows[:, hd:hd + 1], (tm, LANES))
        hi = neg.astype(BF16).astype(F32)
        rest = neg - hi
        mid = rest.astype(BF16).astype(F32)
        lo = rest - mid
        aug = jnp.where(lane < B_DIM, blk,
                        jnp.where(lane == B_DIM, hi,
                                  jnp.where(lane == B_DIM + 1, mid,
                                            jnp.where(lane == B_DIM + 2, lo, 0.0))))
        kbb_o[:, LANES * hd:LANES * (hd + 1)] = aug.astype(BF16)


def _projection(x2, scale, shift, norm_g, w_all, w_zb, wf_t, bf_col, gains, rope_tabs,
                transposed):
    t, d = x2.shape
    tm = PROJ_ROWS
    per_row = scale.shape[0] != 1
    mod_spec = (pl.BlockSpec((tm, d), lambda i: (i, 0)) if per_row
                else pl.BlockSpec((1, d), lambda i: (0, 0)))
    const = lambda shape: pl.BlockSpec(shape, lambda i: (0, 0))
    row = lambda width: pl.BlockSpec((tm, width), lambda i: (i, 0))
    col = lambda height: pl.BlockSpec((height, tm), lambda i: (0, i))
    slab = lambda height: pl.BlockSpec((1, height, tm), lambda i: (i, 0, 0))
    in_specs = [row(d), mod_spec, mod_spec, const((1, d)),
                const(w_all.shape), const(w_zb.shape), const(wf_t.shape), const(bf_col.shape),
                const((GROUP_W, tm)), const((GROUP_W, tm)), const((GROUP_W, tm)),
                const((GROUP_W, tm)),
                col(ROT_DIM // 2), col(ROT_DIM // 2)]
    f32o = jax.ShapeDtypeStruct((t, GROUP_W), F32)
    bf = lambda *shape: jax.ShapeDtypeStruct(shape, BF16)
    va_shape = jax.ShapeDtypeStruct((t * A_HEADS, A_V_DIM), F32)
    va_spec = pl.BlockSpec((tm * A_HEADS, A_V_DIM), lambda i: (i, 0))
    ka_shape = jax.ShapeDtypeStruct((GROUP_W, t), F32) if transposed else f32o
    wide = row(GROUP_W)
    ka_spec = col(GROUP_W) if transposed else wide
    gate = jax.ShapeDtypeStruct((t, GROUP_W), BF16)
    out_shape = ([ka_shape, va_shape, gate, ka_shape, f32o, gate]
                 + [jax.ShapeDtypeStruct((B_HEADS, t), F32)])
    out_specs = [ka_spec, va_spec, wide, ka_spec, wide, wide, col(B_HEADS)]
    if transposed:
        out_shape += [bf(GROUP_W, t), bf(t, GROUP_W), bf(t // tm, A_HEADS * A_VT_ROWS, tm),
                      bf(B_HEADS * LANES, t), bf(t, B_HEADS * LANES),
                      bf(t // tm, B_HEADS * B_VT_ROWS, tm)]
        out_specs += [col(GROUP_W), wide, slab(A_HEADS * A_VT_ROWS),
                      col(B_HEADS * LANES), row(B_HEADS * LANES), slab(B_HEADS * B_VT_ROWS)]
    else:
        out_shape += [bf(t, GROUP_W)] * 6
        out_specs += [wide] * 6
    return pl.pallas_call(
        functools.partial(_proj_kernel, transposed=transposed),
        grid=(t // tm,),
        in_specs=in_specs,
        out_specs=out_specs,
        out_shape=out_shape,
        scratch_shapes=[pltpu.VMEM((16, 1), F32)],
        compiler_params=_params(1),
        name="projection_prompt" if transposed else "projection_decode",
    )(x2, scale, shift, norm_g, w_all, w_zb, wf_t, bf_col, *gains, *rope_tabs)


def _col_max(s):
    groups = s.reshape(8, s.shape[0] // 8, s.shape[1])
    return jnp.max(jnp.max(groups, axis=0), axis=0, keepdims=True)


def _kv_sweep(n_full, streams, last_mask, s_sc, mt_sc, m_sc, acc_sc):
    def produce(t, step, slot, mask=None):
        q_cols, k_of, _ = streams[t]
        s = jnp.dot(k_of(n_full - step), q_cols, preferred_element_type=F32)
        if mask is not None:
            s = jnp.where(mask, s, -jnp.inf)
        s_sc[t, slot] = s
        mt_sc[t, slot] = _col_max(s)

    def consume(t, step, slot):
        v_of = streams[t][2]
        s = s_sc[t, slot]
        m_prev = m_sc[t]
        m_new = jnp.maximum(m_prev, mt_sc[t, slot])
        p = jnp.exp2(s - m_new).astype(BF16)
        pv = None
        for c in range(KV_TILE // SLAB):
            part = jnp.dot(v_of((n_full - step) * (KV_TILE // SLAB) + c),
                           p[SLAB * c:SLAB * (c + 1)], preferred_element_type=F32)
            pv = part if pv is None else pv + part
        acc_sc[t] = jnp.exp2(m_prev - m_new) * acc_sc[t] + pv
        m_sc[t] = m_new

    def stage(produced=None, consumed=None, mask=None):
        for t in range(len(streams)):
            if produced is not None:
                produce(t, *produced, mask)
            if consumed is not None:
                consume(t, *consumed)

    m_sc[...] = jnp.full(m_sc.shape, -jnp.inf, F32)
    acc_sc[...] = jnp.zeros_like(acc_sc)
    def pairs(first, count):
        for k in range(count):
            s = first + 2 * k
            stage(produced=(s + 1, 1), consumed=(s, 0))
            stage(produced=(s + 2, 0), consumed=(s + 1, 1))

    peeled = n_full >= 2

    @pl.when(peeled)
    def _():
        stage(produced=(0, 0), mask=last_mask)
        pairs(0, 1)

    @pl.when(jnp.logical_not(peeled))
    def _():
        stage(produced=(0, 0), mask=last_mask)

    start = jnp.where(peeled, 2, 0)
    rest = n_full - start

    def body4(i, carry):
        pairs(start + 4 * i, 2)
        return carry

    def body2(i, carry):
        pairs(start + 4 * (rest // 4) + 2 * i, 1)
        return carry

    lax.fori_loop(0, rest // 4, body4, 0)
    lax.fori_loop(0, (rest % 4) // 2, body2, 0)

    @pl.when(n_full % 2 == 1)
    def _():
        stage(produced=(n_full, 1), consumed=(n_full - 1, 0))
        stage(consumed=(n_full, 1))

    @pl.when(n_full % 2 == 0)
    def _():
        stage(consumed=(n_full, 0))


def _sweep_scratch(rows):
    n = STEP_STREAMS
    return [pltpu.VMEM((n, 2, KV_TILE, Q_TILE), F32), pltpu.VMEM((n, 2, 1, Q_TILE), F32),
            pltpu.VMEM((n, 1, Q_TILE), F32), pltpu.VMEM((n, rows, Q_TILE), F32)]


def _resident(block_shape, index_map, single):
    mode = pl.Buffered(1) if single else None
    return pl.BlockSpec(block_shape, index_map, pipeline_mode=mode)


def _key_tile(k_ref, j, lanes):
    return k_ref[pl.ds(pl.multiple_of(j * KV_TILE, KV_TILE), KV_TILE), lanes]


def _lam(lq1_ref, lk1_ref, lq2_ref, lk2_ref, lam_init):
    return (jnp.exp(jnp.sum(lq1_ref[...] * lk1_ref[...], axis=1, keepdims=True))
            - jnp.exp(jnp.sum(lq2_ref[...] * lk2_ref[...], axis=1, keepdims=True)) + lam_init)


def _attn_a_kernel(qt_ref, k_ref, vt_ref, z_ref, lq1_ref, lk1_ref, lq2_ref, lk2_ref, sg_ref,
                   o_ref, s_sc, mt_sc, m_sc, acc_sc, *, lam_init):
    qi = pl.program_id(1)
    row = lax.broadcasted_iota(jnp.int32, (LANES, Q_TILE), 0)
    streams = []
    for hd in range(A_STEP_HEADS):
        lanes = slice(LANES * hd, LANES * (hd + 1))
        rows = slice(A_VT_ROWS * hd, A_VT_ROWS * (hd + 1))
        qt = qt_ref[lanes, :]
        zero = jnp.zeros_like(qt)
        k_of = functools.partial(_key_tile, k_ref, lanes=lanes)
        v_of = lambda sl, rows=rows: vt_ref[sl, rows, :]
        streams.append((jnp.where(row < A_QK_DIM, qt, zero), k_of, v_of))
        streams.append((jnp.where(row >= A_QK_DIM, qt, zero), k_of, v_of))
    shape = (KV_TILE, Q_TILE)
    chunk_mask = (lax.broadcasted_iota(jnp.int32, shape, 0) // CHUNK
                  <= lax.broadcasted_iota(jnp.int32, shape, 1) // CHUNK)
    _kv_sweep(qi, streams, chunk_mask, s_sc, mt_sc, m_sc, acc_sc)

    lam = _lam(lq1_ref, lk1_ref, lq2_ref, lk2_ref, lam_init)
    gain = sg_ref[...] * (1.0 - lam_init)
    for hd in range(A_STEP_HEADS):
        lanes = slice(LANES * hd, LANES * (hd + 1))
        a0, a1 = acc_sc[2 * hd], acc_sc[2 * hd + 1]
        oa = ((a0[:A_V_DIM] / a0[A_V_DIM:A_V_DIM + 1])
              - lam * (a1[:A_V_DIM] / a1[A_V_DIM:A_V_DIM + 1]))
        on = oa * lax.rsqrt(jnp.mean(oa * oa, axis=0, keepdims=True) + EPS) * gain
        o_ref[:, lanes] = (on.T * _silu(z_ref[:, lanes].astype(F32))).astype(o_ref.dtype)


def _attn_a_prompt(qa_t, kab, va_t, za, lam_vecs, subln_col, lam_init):
    t = kab.shape[0]
    nh = A_STEP_HEADS
    vec = pl.BlockSpec((1, A_QK_DIM), lambda g, i: (0, 0))
    return pl.pallas_call(
        functools.partial(_attn_a_kernel, lam_init=lam_init),
        grid=(A_HEADS // nh, t // Q_TILE),
        in_specs=[pl.BlockSpec((nh * LANES, Q_TILE), lambda g, i: (g, i)),
                  _resident((t, nh * LANES), lambda g, i: (0, g), False),
                  _resident((t // SLAB, nh * A_VT_ROWS, SLAB), lambda g, i: (0, g, 0), False),
                  pl.BlockSpec((Q_TILE, nh * LANES), lambda g, i: (i, g)),
                  vec, vec, vec, vec,
                  pl.BlockSpec((A_V_DIM, 1), lambda g, i: (0, 0))],
        out_specs=pl.BlockSpec((Q_TILE, nh * LANES), lambda g, i: (i, g)),
        out_shape=jax.ShapeDtypeStruct((t, A_WIDTH), BF16),
        scratch_shapes=_sweep_scratch(A_VT_ROWS),
        compiler_params=_params(2),
        name="attn_a_prompt",
    )(qa_t, kab, va_t, za, *lam_vecs, subln_col)


def _attn_b_kernel(qt_ref, k_ref, vt_ref, z_ref, o_ref, s_sc, mt_sc, m_sc, acc_sc):
    qi = pl.program_id(1)
    shape = (KV_TILE, Q_TILE)
    causal = lax.broadcasted_iota(jnp.int32, shape, 0) <= lax.broadcasted_iota(jnp.int32, shape, 1)
    streams = []
    for t in range(B_STEP_HEADS):
        lanes = slice(LANES * t, LANES * (t + 1))
        rows = slice(B_VT_ROWS * t, B_VT_ROWS * (t + 1))
        streams.append((qt_ref[lanes, :],
                        functools.partial(_key_tile, k_ref, lanes=lanes),
                        lambda sl, rows=rows: vt_ref[sl, rows, :]))
    _kv_sweep(qi, streams, causal, s_sc, mt_sc, m_sc, acc_sc)
    for p in range(B_STEP_HEADS // 2):
        lanes = slice(LANES * p, LANES * (p + 1))
        a0, a1 = acc_sc[2 * p], acc_sc[2 * p + 1]
        o = jnp.concatenate([a0[:B_DIM] / a0[B_DIM:B_DIM + 1], a1[:B_DIM] / a1[B_DIM:B_DIM + 1]],
                            axis=0)
        o_ref[:, lanes] = (o.T * _silu(z_ref[:, lanes].astype(F32))).astype(o_ref.dtype)


def _attn_b_prompt(qb_t, kb_aug, vb_t, zb):
    t = kb_aug.shape[0]
    nh = B_STEP_HEADS
    out_w = nh * B_DIM
    return pl.pallas_call(
        _attn_b_kernel,
        grid=(B_HEADS // nh, t // Q_TILE),
        in_specs=[pl.BlockSpec((nh * LANES, Q_TILE), lambda g, i: (g, i)),
                  _resident((t, nh * LANES), lambda g, i: (0, g), True),
                  _resident((t // SLAB, nh * B_VT_ROWS, SLAB), lambda g, i: (0, g, 0), False),
                  pl.BlockSpec((Q_TILE, out_w), lambda g, i: (i, g))],
        out_specs=pl.BlockSpec((Q_TILE, out_w), lambda g, i: (i, g)),
        out_shape=jax.ShapeDtypeStruct((t, B_WIDTH), BF16),
        scratch_shapes=_sweep_scratch(B_VT_ROWS),
        compiler_params=_params(2),
        name="attn_b_prompt",
    )(qb_t, kb_aug, vb_t, zb)


def _softmax_streams(scores, values):
    ms = [jnp.maximum(jnp.max(s_c, axis=1, keepdims=True), jnp.max(s_n, axis=1, keepdims=True))
          for s_c, s_n in scores]
    ps = [(jnp.exp(s_c - m), jnp.exp(s_n - m)) for (s_c, s_n), m in zip(scores, ms)]
    ls = [jnp.sum(p_c, axis=1, keepdims=True) + jnp.sum(p_n, axis=1, keepdims=True)
          for p_c, p_n in ps]
    accs = [pv_c(p_c.astype(BF16)) + pv_n(p_n.astype(BF16))
            for (p_c, p_n), (pv_c, pv_n) in zip(ps, values)]
    return [acc / l for acc, l in zip(accs, ls)]


def _diff_epilogue(o0, o1, lam, subln_g, lam_init, z):
    oa = o0 - lam * o1
    on = oa * lax.rsqrt(jnp.mean(oa * oa, axis=-1, keepdims=True) + EPS)
    return on * subln_g * (1.0 - lam_init) * _silu(z)


def _attn_dec_kernel(qa_ref, kan_ref, van_ref, za_ref, qb_ref, kbn_ref, vbn_ref, zb_ref,
                     cak_ref, cav_ref, cbk_ref, cbv_ref, lf_ref,
                     lq1_ref, lk1_ref, lq2_ref, lk2_ref, sg_ref,
                     ga_o, gb_o, *, lam_init):
    tdec = qa_ref.shape[0]
    past = cak_ref.shape[2]
    lane = lax.broadcasted_iota(jnp.int32, (tdec, LANES), 1)
    lam = _lam(lq1_ref, lk1_ref, lq2_ref, lk2_ref, lam_init)
    dot = functools.partial(jnp.dot, preferred_element_type=F32)
    dot_nt = lambda a, b: lax.dot_general(a, b, _NT, preferred_element_type=F32)
    scores, values = [], []

    for h in range(A_HEADS):
        cols = slice(LANES * h, LANES * (h + 1))
        q = qa_ref[:, cols]
        zero = jnp.zeros_like(q)
        kt_c = cak_ref[0, cols, :].astype(BF16)
        v_c = cav_ref[0, pl.ds(h, past, stride=A_HEADS), :].astype(BF16)
        k_n = kan_ref[:, cols]
        v_n = van_ref[:, cols]
        for t in range(2):
            qm = jnp.where((lane >= A_QK_DIM) == (t == 1), q, zero)
            scores.append((dot(qm, kt_c), dot_nt(qm, k_n)))
            values.append((lambda p, v=v_c: dot(p, v), lambda p, v=v_n: dot(p, v)))

    lf = lf_ref[0]
    pos = lax.broadcasted_iota(jnp.int32, lf.shape, 1)
    cum = lf
    s = 1
    while s < lf.shape[1]:
        cum = cum + jnp.where(pos >= s, pltpu.roll(cum, s, 1), 0.0)
        s *= 2
    row = lax.broadcasted_iota(jnp.int32, (tdec, tdec), 0)
    col = lax.broadcasted_iota(jnp.int32, (tdec, tdec), 1)
    for p in range(B_HEADS // 2):
        cols = slice(LANES * p, LANES * (p + 1))
        q = qb_ref[:, cols]
        zero = jnp.zeros_like(q)
        kt_c = cbk_ref[0, cols, :].astype(BF16)
        vt_c = cbv_ref[0, cols, :].astype(BF16)
        k_n = kbn_ref[:, cols]
        v_n = vbn_ref[:, cols]
        for t in range(2):
            qm = jnp.where((lane >= B_DIM) == (t == 1), q, zero)
            head = 2 * p + t
            s_c = dot(qm, kt_c) - cum[head:head + 1, :past]
            s_n = dot_nt(qm, k_n) - cum[head:head + 1, past:past + tdec]
            scores.append((s_c, jnp.where(col <= row, s_n, -jnp.inf)))
            values.append((lambda pr, v=vt_c: dot_nt(pr, v), lambda pr, v=v_n: dot(pr, v)))

    outs = _softmax_streams(scores, values)
    for h in range(A_HEADS):
        cols = slice(LANES * h, LANES * (h + 1))
        ga_o[:, cols] = _diff_epilogue(outs[2 * h], outs[2 * h + 1], lam, sg_ref[...], lam_init,
                                       za_ref[:, cols].astype(F32)).astype(ga_o.dtype)
    outs_b = outs[2 * A_HEADS:]
    for p in range(B_HEADS // 2):
        cols = slice(LANES * p, LANES * (p + 1))
        o = jnp.where(lane < B_DIM, outs_b[2 * p], outs_b[2 * p + 1])
        gb_o[:, cols] = (o * _silu(zb_ref[:, cols].astype(F32))).astype(gb_o.dtype)


def _attn_decode(proj, caches, lf_cat, lam_vecs, subln_g, lam_init, nb, tdec):
    cak_t, cav, cbk_t, cbv_t = caches
    past = cak_t.shape[2]
    rows = pl.BlockSpec((tdec, GROUP_W), lambda b: (b, 0))
    feat_major = pl.BlockSpec((1, GROUP_W, past), lambda b: (b, 0, 0))
    vec = pl.BlockSpec((1, A_QK_DIM), lambda b: (0, 0))
    return pl.pallas_call(
        functools.partial(_attn_dec_kernel, lam_init=lam_init),
        grid=(nb,),
        in_specs=[rows] * 8
                 + [feat_major, pl.BlockSpec((1, past * A_HEADS, A_V_DIM), lambda b: (b, 0, 0)),
                    feat_major, feat_major,
                    pl.BlockSpec((1, B_HEADS, lf_cat.shape[2]), lambda b: (b, 0, 0)),
                    vec, vec, vec, vec, pl.BlockSpec((1, A_V_DIM), lambda b: (0, 0))],
        out_specs=[rows, rows],
        out_shape=[jax.ShapeDtypeStruct((nb * tdec, A_WIDTH), BF16),
                   jax.ShapeDtypeStruct((nb * tdec, B_WIDTH), BF16)],
        compiler_params=_params(1),
        name="attn_decode",
    )(*proj, cak_t, cav, cbk_t, cbv_t, lf_cat, *lam_vecs, subln_g)


def _out_kernel(x_ref, gate_ref, ga_ref, gb_ref, w_ref, y_ref):
    o = (jnp.dot(ga_ref[...], w_ref[:A_WIDTH, :], preferred_element_type=F32)
         + jnp.dot(gb_ref[...], w_ref[A_WIDTH:, :], preferred_element_type=F32))
    y_ref[...] = x_ref[...] + gate_ref[...] * o


def _out_projection(x2, gate, ga, gb, w_out):
    t, d = x2.shape
    tm = min(OUT_ROWS, t)
    gate_spec = (pl.BlockSpec((tm, d), lambda i: (i, 0)) if gate.shape[0] != 1
                 else pl.BlockSpec((1, d), lambda i: (0, 0)))
    return pl.pallas_call(
        _out_kernel,
        grid=(t // tm,),
        in_specs=[pl.BlockSpec((tm, d), lambda i: (i, 0)), gate_spec,
                  pl.BlockSpec((tm, A_WIDTH), lambda i: (i, 0)),
                  pl.BlockSpec((tm, B_WIDTH), lambda i: (i, 0)),
                  pl.BlockSpec(w_out.shape, lambda i: (0, 0))],
        out_specs=pl.BlockSpec((tm, d), lambda i: (i, 0)),
        out_shape=jax.ShapeDtypeStruct((t, d), F32),
        compiler_params=_params(1),
        name="out_projection",
    )(x2, gate, ga, gb, w_out)


def _rope_tables(pos):
    inv = ROPE_THETA ** (-jnp.arange(0, ROT_DIM, 2, dtype=F32) / ROT_DIM)
    ang = (pos.astype(F32)[:, None] * inv[None, :]).T
    return jnp.cos(ang), jnp.sin(ang)


def kernel(x_prompt, x_sample, cache_a_k, cache_a_v, cache_b_k, cache_b_v, cache_b_logf,
           c_prompt, c_sample, norm_g, w_ada, b_ada, w_in, b_f, qn_a, kn_a,
           lam_q1, lam_k1, lam_q2, lam_k2, subln_g, qn_b, kn_b, w_out):
    depth = norm_g.shape[0]
    bp, seq, d = x_prompt.shape
    bs, tdec, _ = x_sample.shape
    past = cache_a_k.shape[2]
    assert bp == 1 and seq % KV_TILE == 0 and Q_TILE % CHUNK == 0 and seq % OUT_ROWS == 0
    assert past % CHUNK == 0 and tdec == CHUNK and (bs * tdec) % PROJ_ROWS == 0

    xp = x_prompt.reshape(seq, d)
    xs = x_sample.reshape(bs * tdec, d)
    c_rows = bp + bs
    c_all = jnp.concatenate([c_prompt, c_sample, jnp.zeros((16 - c_rows, d), F32)], axis=0)
    rope_p = _rope_tables(jnp.arange(seq))
    rope_s = _rope_tables(jnp.tile(past + jnp.arange(tdec), bs))
    fb_off = 2 * A_QK_WIDTH + 2 * A_WIDTH + 3 * B_WIDTH

    outs_p = [[] for _ in range(5)]
    outs_s = [[] for _ in range(5)]
    for l in range(depth):
        lam_init = 0.8 - 0.6 * math.exp(-0.3 * l)
        w_all = w_in[l].astype(BF16)
        w_zb = w_all[:, fb_off + B_HEADS:]
        wf_t = jnp.concatenate([w_in[l][:, fb_off:fb_off + B_HEADS].T.astype(BF16),
                                jnp.zeros((16 - B_HEADS, d), BF16)], axis=0)
        bf_col = jnp.concatenate([b_f[l], jnp.zeros((16 - B_HEADS,), F32)]).reshape(16, 1)
        gains = tuple(jnp.broadcast_to(jnp.tile(g[l], GROUP_W // g.shape[1])[:, None],
                                       (GROUP_W, PROJ_ROWS))
                      for g in (qn_a, kn_a, qn_b, kn_b))
        lam_vecs = tuple(v[l].reshape(1, A_QK_DIM) for v in (lam_q1, lam_k1, lam_q2, lam_k2))
        w_o = w_out[l].astype(BF16)
        g_row = norm_g[l].reshape(1, d)

        mod = _modulation(c_all, w_ada[l], b_ada[l])
        shift, scale, gate = mod[:, :d], mod[:, d:2 * d], mod[:, 2 * d:]

        (ka_t, va, za, kb_t, vb, zb, lf_t, qa_t, kab, va_t, qb_t, kb_aug, vb_t) = _projection(
            xp, scale[:bp], shift[:bp], g_row, w_all, w_zb, wf_t, bf_col, gains, rope_p, True)
        ga = _attn_a_prompt(qa_t, kab, va_t, za, lam_vecs, subln_g[l].reshape(A_V_DIM, 1), lam_init)
        gb = _attn_b_prompt(qb_t, kb_aug, vb_t, zb)
        xp = _out_projection(xp, gate[:bp], ga, gb, w_o)
        ka = ka_t.reshape(A_HEADS, 2, A_QK_DIM, seq).transpose(3, 0, 1, 2)
        kb = kb_t.reshape(B_HEADS, B_DIM, seq).transpose(2, 0, 1)
        for dst, a in zip(outs_p, (ka.reshape(bp, seq, A_HEADS, 2, A_QK_DIM),
                                   va.reshape(bp, seq, A_HEADS, A_V_DIM),
                                   kb.reshape(bp, seq, B_HEADS, B_DIM),
                                   vb.reshape(bp, seq, B_HEADS, B_DIM),
                                   lf_t.T.reshape(bp, seq, B_HEADS))):
            dst.append(a)

        rep = lambda a: jnp.repeat(a[bp:c_rows], tdec, axis=0)
        (ka, va, za, kb, vb, zb, lf_t, qa, kab, vab, qb, kbb, vbb) = _projection(
            xs, rep(scale), rep(shift), g_row, w_all, w_zb, wf_t, bf_col, gains, rope_s, False)
        lf_new = lf_t.reshape(B_HEADS, bs, tdec).transpose(1, 0, 2)
        lf_cat = jnp.concatenate([cache_b_logf[l].astype(F32).transpose(0, 2, 1), lf_new,
                                  jnp.zeros((bs, B_HEADS, LANES - tdec), F32)], axis=2)
        caches = (cache_a_k[l].transpose(0, 2, 3, 4, 1).reshape(bs, A_QK_WIDTH, past),
                  cache_a_v[l].reshape(bs, past * A_HEADS, A_V_DIM),
                  cache_b_k[l].transpose(0, 2, 3, 1).reshape(bs, B_WIDTH, past),
                  cache_b_v[l].transpose(0, 2, 3, 1).reshape(bs, B_WIDTH, past))
        ga, gb = _attn_decode((qa, kab, vab, za, qb, kbb, vbb, zb), caches, lf_cat,
                              lam_vecs, subln_g[l].reshape(1, A_V_DIM), lam_init, bs, tdec)
        xs = _out_projection(xs, rep(gate), ga, gb, w_o)
        for dst, a in zip(outs_s, (ka.reshape(bs, tdec, A_HEADS, 2, A_QK_DIM),
                                   va.reshape(bs, tdec, A_HEADS, A_V_DIM),
                                   kb.reshape(bs, tdec, B_HEADS, B_DIM),
                                   vb.reshape(bs, tdec, B_HEADS, B_DIM),
                                   lf_t.T.reshape(bs, tdec, B_HEADS))):
            dst.append(a)

    return (xp.reshape(bp, seq, d), xs.reshape(bs, tdec, d),
            *(jnp.stack(o) for o in outs_p), *(jnp.stack(o) for o in outs_s))
```

```python
import functools
import math

import jax
import jax.numpy as jnp
from jax import lax
from jax.experimental import pallas as pl
from jax.experimental.pallas import tpu as pltpu

F32 = jnp.float32
BF16 = jnp.bfloat16

CHUNK = 64
A_HEADS = 4
A_QK_DIM = 64
A_V_DIM = 2 * A_QK_DIM
B_HEADS = 8
B_DIM = 64
A_WIDTH = A_HEADS * A_V_DIM
B_WIDTH = B_HEADS * B_DIM
A_QK_WIDTH = A_HEADS * 2 * A_QK_DIM
ROT_DIM = A_QK_DIM // 4
ROPE_THETA = 500000.0
EPS = 1e-6
LOG2_E = math.log2(math.e)

LANES = 128
BF16_ROWS = 16
MXU_DIM = 256
VMEM_LIMIT = 56 * 1024 * 1024

SLAB = MXU_DIM
PROJ_ROWS = SLAB
KV_TILE = 2 * SLAB
Q_TILE = KV_TILE
UNROLL_PAIRS = 4
STEP_STREAMS = 4
A_STEP_HEADS = STEP_STREAMS // 2
B_STEP_HEADS = STEP_STREAMS
OUT_ROWS = 1024

A_VT_ROWS = A_V_DIM + BF16_ROWS
B_VT_ROWS = B_DIM + BF16_ROWS
B_BIAS_PIECES = 3
N_GROUPS = 8
GROUP_W = A_QK_WIDTH
assert GROUP_W == A_WIDTH == B_WIDTH

_NT = (((1,), (1,)), ((), ()))


def _params(n_axes):
    return pltpu.CompilerParams(dimension_semantics=("arbitrary",) * n_axes,
                                vmem_limit_bytes=VMEM_LIMIT)


def _silu(x):
    return x * jax.nn.sigmoid(x)


def _mod_kernel(c_ref, w_ref, b_ref, o_ref):
    a = _silu(c_ref[...]).astype(BF16)
    o_ref[...] = jnp.dot(a, w_ref[...].astype(BF16), preferred_element_type=F32) + b_ref[...]


def _modulation(c_all, w_ada, b_ada):
    rows, d = c_all.shape
    n = w_ada.shape[1]
    bn = GROUP_W
    return pl.pallas_call(
        _mod_kernel,
        grid=(n // bn,),
        in_specs=[pl.BlockSpec((rows, d), lambda j: (0, 0)),
                  pl.BlockSpec((d, bn), lambda j: (0, j)),
                  pl.BlockSpec((1, bn), lambda j: (0, j))],
        out_specs=pl.BlockSpec((rows, bn), lambda j: (0, j)),
        out_shape=jax.ShapeDtypeStruct((rows, n), F32),
        compiler_params=_params(1),
        name="modulation",
    )(c_all, w_ada, b_ada.reshape(1, n))


def _group_rms_t(u, gain_t):
    ut = u.T
    groups = ut.reshape(ut.shape[0] // A_QK_DIM, A_QK_DIM, ut.shape[1])
    ms = jnp.mean(groups * groups, axis=1, keepdims=True)
    return (groups * lax.rsqrt(ms + EPS)).reshape(ut.shape) * gain_t


def _rope_t(xt, cos_t, sin_t):
    half = ROT_DIM // 2
    pieces = []
    for base in range(0, xt.shape[0], A_QK_DIM):
        x1 = xt[base:base + half]
        x2 = xt[base + half:base + ROT_DIM]
        pieces += [x1 * cos_t - x2 * sin_t, x2 * cos_t + x1 * sin_t,
                   xt[base + ROT_DIM:base + A_QK_DIM]]
    return jnp.concatenate(pieces, axis=0)


def _with_ones_rows(xt, rows, n_ones):
    r = lax.broadcasted_iota(jnp.int32, (rows, xt.shape[1]), 0)
    return jnp.concatenate([xt, jnp.where(r < n_ones, 1.0, 0.0)], axis=0)


def _proj_kernel(x_ref, scale_ref, shift_ref, g_ref, w_ref, wz_ref, wf_ref, bf_ref,
                 qna_ref, kna_ref, qnb_ref, knb_ref, cos_ref, sin_ref,
                 ka_o, va_o, za_o, kb_o, vb_o, zb_o, lf_o,
                 qa_o, kab_o, vab_o, qb_o, kbb_o, vbb_o, carry_ref, *, transposed):
    @pl.when(pl.program_id(0) == 0)
    def _():
        carry_ref[...] = jnp.zeros_like(carry_ref)

    tm = x_ref.shape[0]
    x = x_ref[...]
    y = x * lax.rsqrt(jnp.mean(x * x, axis=-1, keepdims=True) + EPS) * g_ref[...]
    h = y * (1.0 + scale_ref[...]) + shift_ref[...]
    hb = h.astype(BF16)
    cos_t, sin_t = cos_ref[...], sin_ref[...]
    qk_scale = A_QK_DIM ** -0.5 * (LOG2_E if transposed else 1.0)
    lane = lax.broadcasted_iota(jnp.int32, (tm, LANES), 1)

    def group(i):
        w = wz_ref[...] if i == N_GROUPS - 1 else w_ref[:, GROUP_W * i:GROUP_W * (i + 1)]
        return jnp.dot(hb, w, preferred_element_type=F32)

    fb = lax.dot_general(wf_ref[...], hb, _NT, preferred_element_type=F32)
    fx = fb + bf_ref[...]
    logf = jnp.minimum(fx, 0.0) - jnp.log1p(jnp.exp(-jnp.abs(fx)))
    lf_o[...] = logf[:B_HEADS]

    qa_t = _rope_t(_group_rms_t(group(0), qna_ref[...]), cos_t, sin_t) * qk_scale
    ka_t = _rope_t(_group_rms_t(group(1), kna_ref[...]), cos_t, sin_t)
    ka = ka_t.T
    ka_o[...] = ka_t if transposed else ka
    kab_o[...] = ka.astype(BF16)
    va = group(2)
    for hd in range(A_HEADS):
        va_o[pl.ds(hd, tm, stride=A_HEADS), :] = va[:, A_V_DIM * hd:A_V_DIM * (hd + 1)]
    za_o[...] = group(3).astype(BF16)
    qb_t = _group_rms_t(group(4), qnb_ref[...]) * qk_scale
    kb_t = _group_rms_t(group(5), knb_ref[...])
    kb = kb_t.T
    kb_o[...] = kb_t if transposed else kb
    vb = group(6)
    vb_o[...] = vb
    zb_o[...] = group(7).astype(BF16)

    if not transposed:
        qa_o[...] = qa_t.T.astype(BF16)
        vab_o[...] = va.astype(BF16)
        qb_o[...] = qb_t.T.astype(BF16)
        kbb_o[...] = kb.astype(BF16)
        vbb_o[...] = vb.astype(BF16)
        return

    qa_o[...] = qa_t.astype(BF16)
    va_t = va.T
    for hd in range(A_HEADS):
        blk = _with_ones_rows(va_t[A_V_DIM * hd:A_V_DIM * (hd + 1)], BF16_ROWS, 1)
        vab_o[0, A_VT_ROWS * hd:A_VT_ROWS * (hd + 1), :] = blk.astype(BF16)
    vb_t = vb.T
    for hd in range(B_HEADS):
        blk = _with_ones_rows(qb_t[B_DIM * hd:B_DIM * (hd + 1)], LANES - B_DIM, B_BIAS_PIECES)
        qb_o[LANES * hd:LANES * (hd + 1), :] = blk.astype(BF16)
        blk = _with_ones_rows(vb_t[B_DIM * hd:B_DIM * (hd + 1)], BF16_ROWS, 1)
        vbb_o[0, B_VT_ROWS * hd:B_VT_ROWS * (hd + 1), :] = blk.astype(BF16)

    pos = lax.broadcasted_iota(jnp.int32, logf.shape, 1)
    c = logf
    s = 1
    while s < tm:
        c = c + jnp.where(pos >= s, pltpu.roll(c, s, 1), 0.0)
        s *= 2
    c = c + carry_ref[...]
    carry_ref[...] = c[:, tm - 1:tm]

    cum_rows = jnp.concatenate([c, jnp.zeros((LANES - c.shape[0], tm), F32)], axis=0).T
    for hd in range(B_HEADS):
        blk = kb[:, LANES * (hd // 2):LANES * (hd // 2 + 1)]
        if hd % 2:
            blk = pltpu.roll(blk, B_DIM, 1)
        neg = -LOG2_E * jnp.broadcast_to(cum_rows[:, hd:hd + 1], (tm, LANES))
        hi = neg.astype(BF16).astype(F32)
        rest = neg - hi
        mid = rest.astype(BF16).astype(F32)
        lo = rest - mid
        aug = jnp.where(lane < B_DIM, blk,
                        jnp.where(lane == B_DIM, hi,
                                  jnp.where(lane == B_DIM + 1, mid,
                                            jnp.where(lane == B_DIM + 2, lo, 0.0))))
        kbb_o[:, LANES * hd:LANES * (hd + 1)] = aug.astype(BF16)


def _projection(x2, scale, shift, norm_g, w_all, w_zb, wf_t, bf_col, gains, rope_tabs,
                transposed):
    t, d = x2.shape
    tm = PROJ_ROWS
    per_row = scale.shape[0] != 1
    mod_spec = (pl.BlockSpec((tm, d), lambda i: (i, 0)) if per_row
                else pl.BlockSpec((1, d), lambda i: (0, 0)))
    const = lambda shape: pl.BlockSpec(shape, lambda i: (0, 0))
    row = lambda width: pl.BlockSpec((tm, width), lambda i: (i, 0))
    col = lambda height: pl.BlockSpec((height, tm), lambda i: (0, i))
    slab = lambda height: pl.BlockSpec((1, height, tm), lambda i: (i, 0, 0))
    in_specs = [row(d), mod_spec, mod_spec, const((1, d)),
                const(w_all.shape), const(w_zb.shape), const(wf_t.shape), const(bf_col.shape),
                const((GROUP_W, tm)), const((GROUP_W, tm)), const((GROUP_W, tm)),
                const((GROUP_W, tm)),
                col(ROT_DIM // 2), col(ROT_DIM // 2)]
    f32o = jax.ShapeDtypeStruct((t, GROUP_W), F32)
    bf = lambda *shape: jax.ShapeDtypeStruct(shape, BF16)
    va_shape = jax.ShapeDtypeStruct((t * A_HEADS, A_V_DIM), F32)
    va_spec = pl.BlockSpec((tm * A_HEADS, A_V_DIM), lambda i: (i, 0))
    ka_shape = jax.ShapeDtypeStruct((GROUP_W, t), F32) if transposed else f32o
    wide = row(GROUP_W)
    ka_spec = col(GROUP_W) if transposed else wide
    gate = jax.ShapeDtypeStruct((t, GROUP_W), BF16)
    out_shape = ([ka_shape, va_shape, gate, ka_shape, f32o, gate]
                 + [jax.ShapeDtypeStruct((B_HEADS, t), F32)])
    out_specs = [ka_spec, va_spec, wide, ka_spec, wide, wide, col(B_HEADS)]
    if transposed:
        out_shape += [bf(GROUP_W, t), bf(t, GROUP_W), bf(t // tm, A_HEADS * A_VT_ROWS, tm),
                      bf(B_HEADS * LANES, t), bf(t, B_HEADS * LANES),
                      bf(t // tm, B_HEADS * B_VT_ROWS, tm)]
        out_specs += [col(GROUP_W), wide, slab(A_HEADS * A_VT_ROWS),
                      col(B_HEADS * LANES), row(B_HEADS * LANES), slab(B_HEADS * B_VT_ROWS)]
    else:
        out_shape += [bf(t, GROUP_W)] * 6
        out_specs += [wide] * 6
    return pl.pallas_call(
        functools.partial(_proj_kernel, transposed=transposed),
        grid=(t // tm,),
        in_specs=in_specs,
        out_specs=out_specs,
        out_shape=out_shape,
        scratch_shapes=[pltpu.VMEM((16, 1), F32)],
        compiler_params=_params(1),
        name="projection_prompt" if transposed else "projection_decode",
    )(x2, scale, shift, norm_g, w_all, w_zb, wf_t, bf_col, *gains, *rope_tabs)


def _col_max(s):
    groups = s.reshape(8, s.shape[0] // 8, s.shape[1])
    return jnp.max(jnp.max(groups, axis=0), axis=0, keepdims=True)


def _kv_sweep(n_full, streams, last_mask, s_sc, mt_sc, m_sc, acc_sc, finish):
    def produce(t, step, slot, mask=None):
        q_cols, k_of, _ = streams[t]
        s = jnp.dot(k_of(n_full - step), q_cols, preferred_element_type=F32)
        if mask is not None:
            s = jnp.where(mask, s, -jnp.inf)
        s_sc[t, slot] = s
        mt_sc[t, slot] = _col_max(s)

    def consume(t, step, slot):
        v_of = streams[t][2]
        s = s_sc[t, slot]
        m_prev = m_sc[t]
        m_new = jnp.maximum(m_prev, mt_sc[t, slot])
        p = jnp.exp2(s - m_new).astype(BF16)
        pv = None
        for c in range(KV_TILE // SLAB):
            part = jnp.dot(v_of((n_full - step) * (KV_TILE // SLAB) + c),
                           p[SLAB * c:SLAB * (c + 1)], preferred_element_type=F32)
            pv = part if pv is None else pv + part
        acc_sc[t] = jnp.exp2(m_prev - m_new) * acc_sc[t] + pv
        m_sc[t] = m_new

    def stage(produced=None, consumed=None, mask=None, last=False):
        for t in range(len(streams)):
            if produced is not None:
                produce(t, *produced, mask)
            if consumed is not None:
                consume(t, *consumed)
            if last:
                finish(t)

    m_sc[...] = jnp.full(m_sc.shape, -jnp.inf, F32)
    acc_sc[...] = jnp.zeros_like(acc_sc)
    def pairs(first, count):
        for k in range(count):
            s = first + 2 * k
            stage(produced=(s + 1, 1), consumed=(s, 0))
            stage(produced=(s + 2, 0), consumed=(s + 1, 1))

    peeled = n_full >= 2

    @pl.when(peeled)
    def _():
        stage(produced=(0, 0), mask=last_mask)
        pairs(0, 1)

    @pl.when(jnp.logical_not(peeled))
    def _():
        stage(produced=(0, 0), mask=last_mask)

    start = jnp.where(peeled, 2, 0)
    rest = n_full - start

    span = 2 * UNROLL_PAIRS

    def body_main(i, carry):
        pairs(start + span * i, UNROLL_PAIRS)
        return carry

    def body_rest(i, carry):
        pairs(start + span * (rest // span) + 2 * i, 1)
        return carry

    lax.fori_loop(0, rest // span, body_main, 0)
    lax.fori_loop(0, (rest % span) // 2, body_rest, 0)

    @pl.when(n_full % 2 == 1)
    def _():
        stage(produced=(n_full, 1), consumed=(n_full - 1, 0))
        stage(consumed=(n_full, 1), last=True)

    @pl.when(n_full % 2 == 0)
    def _():
        stage(consumed=(n_full, 0), last=True)


def _sweep_scratch(rows):
    n = STEP_STREAMS
    return [pltpu.VMEM((n, 2, KV_TILE, Q_TILE), F32), pltpu.VMEM((n, 2, 1, Q_TILE), F32),
            pltpu.VMEM((n, 1, Q_TILE), F32), pltpu.VMEM((n, rows, Q_TILE), F32)]


def _resident(block_shape, index_map, single):
    mode = pl.Buffered(1) if single else None
    return pl.BlockSpec(block_shape, index_map, pipeline_mode=mode)


def _key_tile(k_ref, j, lanes):
    return k_ref[pl.ds(pl.multiple_of(j * KV_TILE, KV_TILE), KV_TILE), lanes]


def _lam(lq1_ref, lk1_ref, lq2_ref, lk2_ref, lam_init):
    return (jnp.exp(jnp.sum(lq1_ref[...] * lk1_ref[...], axis=1, keepdims=True))
            - jnp.exp(jnp.sum(lq2_ref[...] * lk2_ref[...], axis=1, keepdims=True)) + lam_init)


def _attn_a_kernel(qt_ref, k_ref, vt_ref, z_ref, lq1_ref, lk1_ref, lq2_ref, lk2_ref, sg_ref,
                   o_ref, s_sc, mt_sc, m_sc, acc_sc, *, lam_init):
    qi = pl.program_id(1)
    row = lax.broadcasted_iota(jnp.int32, (LANES, Q_TILE), 0)
    streams = []
    for hd in range(A_STEP_HEADS):
        lanes = slice(LANES * hd, LANES * (hd + 1))
        rows = slice(A_VT_ROWS * hd, A_VT_ROWS * (hd + 1))
        qt = qt_ref[lanes, :]
        zero = jnp.zeros_like(qt)
        k_of = functools.partial(_key_tile, k_ref, lanes=lanes)
        v_of = lambda sl, rows=rows: vt_ref[sl, rows, :]
        streams.append((jnp.where(row < A_QK_DIM, qt, zero), k_of, v_of))
        streams.append((jnp.where(row >= A_QK_DIM, qt, zero), k_of, v_of))
    shape = (KV_TILE, Q_TILE)
    chunk_mask = (lax.broadcasted_iota(jnp.int32, shape, 0) // CHUNK
                  <= lax.broadcasted_iota(jnp.int32, shape, 1) // CHUNK)
    lam = _lam(lq1_ref, lk1_ref, lq2_ref, lk2_ref, lam_init)
    gain = sg_ref[...] * (1.0 - lam_init)

    def finish(t):
        if t % 2 == 0:
            return
        lanes = slice(LANES * (t // 2), LANES * (t // 2 + 1))
        a0, a1 = acc_sc[t - 1], acc_sc[t]
        oa = ((a0[:A_V_DIM] / a0[A_V_DIM:A_V_DIM + 1])
              - lam * (a1[:A_V_DIM] / a1[A_V_DIM:A_V_DIM + 1]))
        on = oa * lax.rsqrt(jnp.mean(oa * oa, axis=0, keepdims=True) + EPS) * gain
        o_ref[:, lanes] = (on.T * _silu(z_ref[:, lanes].astype(F32))).astype(o_ref.dtype)

    _kv_sweep(qi, streams, chunk_mask, s_sc, mt_sc, m_sc, acc_sc, finish)


def _attn_a_prompt(qa_t, kab, va_t, za, lam_vecs, subln_col, lam_init):
    t = kab.shape[0]
    nh = A_STEP_HEADS
    vec = pl.BlockSpec((1, A_QK_DIM), lambda g, i: (0, 0))
    return pl.pallas_call(
        functools.partial(_attn_a_kernel, lam_init=lam_init),
        grid=(A_HEADS // nh, t // Q_TILE),
        in_specs=[pl.BlockSpec((nh * LANES, Q_TILE), lambda g, i: (g, i)),
                  _resident((t, nh * LANES), lambda g, i: (0, g), False),
                  _resident((t // SLAB, nh * A_VT_ROWS, SLAB), lambda g, i: (0, g, 0), False),
                  pl.BlockSpec((Q_TILE, nh * LANES), lambda g, i: (i, g)),
                  vec, vec, vec, vec,
                  pl.BlockSpec((A_V_DIM, 1), lambda g, i: (0, 0))],
        out_specs=pl.BlockSpec((Q_TILE, nh * LANES), lambda g, i: (i, g)),
        out_shape=jax.ShapeDtypeStruct((t, A_WIDTH), BF16),
        scratch_shapes=_sweep_scratch(A_VT_ROWS),
        compiler_params=_params(2),
        name="attn_a_prompt",
    )(qa_t, kab, va_t, za, *lam_vecs, subln_col)


def _attn_b_kernel(qt_ref, k_ref, vt_ref, z_ref, o_ref, s_sc, mt_sc, m_sc, acc_sc):
    qi = pl.program_id(1)
    shape = (KV_TILE, Q_TILE)
    causal = lax.broadcasted_iota(jnp.int32, shape, 0) <= lax.broadcasted_iota(jnp.int32, shape, 1)
    streams = []
    for t in range(B_STEP_HEADS):
        lanes = slice(LANES * t, LANES * (t + 1))
        rows = slice(B_VT_ROWS * t, B_VT_ROWS * (t + 1))
        streams.append((qt_ref[lanes, :],
                        functools.partial(_key_tile, k_ref, lanes=lanes),
                        lambda sl, rows=rows: vt_ref[sl, rows, :]))

    def finish(t):
        if t % 2 == 0:
            return
        lanes = slice(LANES * (t // 2), LANES * (t // 2 + 1))
        a0, a1 = acc_sc[t - 1], acc_sc[t]
        o = jnp.concatenate([a0[:B_DIM] / a0[B_DIM:B_DIM + 1], a1[:B_DIM] / a1[B_DIM:B_DIM + 1]],
                            axis=0)
        o_ref[:, lanes] = (o.T * _silu(z_ref[:, lanes].astype(F32))).astype(o_ref.dtype)

    _kv_sweep(qi, streams, causal, s_sc, mt_sc, m_sc, acc_sc, finish)


def _attn_b_prompt(qb_t, kb_aug, vb_t, zb):
    t = kb_aug.shape[0]
    nh = B_STEP_HEADS
    out_w = nh * B_DIM
    return pl.pallas_call(
        _attn_b_kernel,
        grid=(B_HEADS // nh, t // Q_TILE),
        in_specs=[pl.BlockSpec((nh * LANES, Q_TILE), lambda g, i: (g, i)),
                  _resident((t, nh * LANES), lambda g, i: (0, g), True),
                  _resident((t // SLAB, nh * B_VT_ROWS, SLAB), lambda g, i: (0, g, 0), False),
                  pl.BlockSpec((Q_TILE, out_w), lambda g, i: (i, g))],
        out_specs=pl.BlockSpec((Q_TILE, out_w), lambda g, i: (i, g)),
        out_shape=jax.ShapeDtypeStruct((t, B_WIDTH), BF16),
        scratch_shapes=_sweep_scratch(B_VT_ROWS),
        compiler_params=_params(2),
        name="attn_b_prompt",
    )(qb_t, kb_aug, vb_t, zb)


def _softmax_streams(scores, values):
    ms = [jnp.maximum(jnp.max(s_c, axis=1, keepdims=True), jnp.max(s_n, axis=1, keepdims=True))
          for s_c, s_n in scores]
    ps = [(jnp.exp(s_c - m), jnp.exp(s_n - m)) for (s_c, s_n), m in zip(scores, ms)]
    ls = [jnp.sum(p_c, axis=1, keepdims=True) + jnp.sum(p_n, axis=1, keepdims=True)
          for p_c, p_n in ps]
    accs = [pv_c(p_c.astype(BF16)) + pv_n(p_n.astype(BF16))
            for (p_c, p_n), (pv_c, pv_n) in zip(ps, values)]
    return [acc / l for acc, l in zip(accs, ls)]


def _diff_epilogue(o0, o1, lam, subln_g, lam_init, z):
    oa = o0 - lam * o1
    on = oa * lax.rsqrt(jnp.mean(oa * oa, axis=-1, keepdims=True) + EPS)
    return on * subln_g * (1.0 - lam_init) * _silu(z)


def _attn_dec_kernel(qa_ref, kan_ref, van_ref, za_ref, qb_ref, kbn_ref, vbn_ref, zb_ref,
                     cak_ref, cav_ref, cbk_ref, cbv_ref, lf_ref,
                     lq1_ref, lk1_ref, lq2_ref, lk2_ref, sg_ref,
                     ga_o, gb_o, *, lam_init):
    tdec = qa_ref.shape[0]
    past = cak_ref.shape[2]
    lane = lax.broadcasted_iota(jnp.int32, (tdec, LANES), 1)
    lam = _lam(lq1_ref, lk1_ref, lq2_ref, lk2_ref, lam_init)
    dot = functools.partial(jnp.dot, preferred_element_type=F32)
    dot_nt = lambda a, b: lax.dot_general(a, b, _NT, preferred_element_type=F32)
    scores, values = [], []

    for h in range(A_HEADS):
        cols = slice(LANES * h, LANES * (h + 1))
        q = qa_ref[:, cols]
        zero = jnp.zeros_like(q)
        kt_c = cak_ref[0, cols, :].astype(BF16)
        v_c = cav_ref[0, pl.ds(h, past, stride=A_HEADS), :].astype(BF16)
        k_n = kan_ref[:, cols]
        v_n = van_ref[:, cols]
        for t in range(2):
            qm = jnp.where((lane >= A_QK_DIM) == (t == 1), q, zero)
            scores.append((dot(qm, kt_c), dot_nt(qm, k_n)))
            values.append((lambda p, v=v_c: dot(p, v), lambda p, v=v_n: dot(p, v)))

    lf = lf_ref[0]
    pos = lax.broadcasted_iota(jnp.int32, lf.shape, 1)
    cum = lf
    s = 1
    while s < lf.shape[1]:
        cum = cum + jnp.where(pos >= s, pltpu.roll(cum, s, 1), 0.0)
        s *= 2
    row = lax.broadcasted_iota(jnp.int32, (tdec, tdec), 0)
    col = lax.broadcasted_iota(jnp.int32, (tdec, tdec), 1)
    for p in range(B_HEADS // 2):
        cols = slice(LANES * p, LANES * (p + 1))
        q = qb_ref[:, cols]
        zero = jnp.zeros_like(q)
        kt_c = cbk_ref[0, cols, :].astype(BF16)
        vt_c = cbv_ref[0, cols, :].astype(BF16)
        k_n = kbn_ref[:, cols]
        v_n = vbn_ref[:, cols]
        for t in range(2):
            qm = jnp.where((lane >= B_DIM) == (t == 1), q, zero)
            head = 2 * p + t
            s_c = dot(qm, kt_c) - cum[head:head + 1, :past]
            s_n = dot_nt(qm, k_n) - cum[head:head + 1, past:past + tdec]
            scores.append((s_c, jnp.where(col <= row, s_n, -jnp.inf)))
            values.append((lambda pr, v=vt_c: dot_nt(pr, v), lambda pr, v=v_n: dot(pr, v)))

    outs = _softmax_streams(scores, values)
    for h in range(A_HEADS):
        cols = slice(LANES * h, LANES * (h + 1))
        ga_o[:, cols] = _diff_epilogue(outs[2 * h], outs[2 * h + 1], lam, sg_ref[...], lam_init,
                                       za_ref[:, cols].astype(F32)).astype(ga_o.dtype)
    outs_b = outs[2 * A_HEADS:]
    for p in range(B_HEADS // 2):
        cols = slice(LANES * p, LANES * (p + 1))
        o = jnp.where(lane < B_DIM, outs_b[2 * p], outs_b[2 * p + 1])
        gb_o[:, cols] = (o * _silu(zb_ref[:, cols].astype(F32))).astype(gb_o.dtype)


def _attn_decode(proj, caches, lf_cat, lam_vecs, subln_g, lam_init, nb, tdec):
    cak_t, cav, cbk_t, cbv_t = caches
    past = cak_t.shape[2]
    rows = pl.BlockSpec((tdec, GROUP_W), lambda b: (b, 0))
    feat_major = pl.BlockSpec((1, GROUP_W, past), lambda b: (b, 0, 0))
    vec = pl.BlockSpec((1, A_QK_DIM), lambda b: (0, 0))
    return pl.pallas_call(
        functools.partial(_attn_dec_kernel, lam_init=lam_init),
        grid=(nb,),
        in_specs=[rows] * 8
                 + [feat_major, pl.BlockSpec((1, past * A_HEADS, A_V_DIM), lambda b: (b, 0, 0)),
                    feat_major, feat_major,
                    pl.BlockSpec((1, B_HEADS, lf_cat.shape[2]), lambda b: (b, 0, 0)),
                    vec, vec, vec, vec, pl.BlockSpec((1, A_V_DIM), lambda b: (0, 0))],
        out_specs=[rows, rows],
        out_shape=[jax.ShapeDtypeStruct((nb * tdec, A_WIDTH), BF16),
                   jax.ShapeDtypeStruct((nb * tdec, B_WIDTH), BF16)],
        compiler_params=_params(1),
        name="attn_decode",
    )(*proj, cak_t, cav, cbk_t, cbv_t, lf_cat, *lam_vecs, subln_g)


def _out_kernel(x_ref, gate_ref, ga_ref, gb_ref, w_ref, y_ref):
    o = (jnp.dot(ga_ref[...], w_ref[:A_WIDTH, :], preferred_element_type=F32)
         + jnp.dot(gb_ref[...], w_ref[A_WIDTH:, :], preferred_element_type=F32))
    y_ref[...] = x_ref[...] + gate_ref[...] * o


def _out_projection(x2, gate, ga, gb, w_out):
    t, d = x2.shape
    tm = min(OUT_ROWS, t)
    gate_spec = (pl.BlockSpec((tm, d), lambda i: (i, 0)) if gate.shape[0] != 1
                 else pl.BlockSpec((1, d), lambda i: (0, 0)))
    return pl.pallas_call(
        _out_kernel,
        grid=(t // tm,),
        in_specs=[pl.BlockSpec((tm, d), lambda i: (i, 0)), gate_spec,
                  pl.BlockSpec((tm, A_WIDTH), lambda i: (i, 0)),
                  pl.BlockSpec((tm, B_WIDTH), lambda i: (i, 0)),
                  pl.BlockSpec(w_out.shape, lambda i: (0, 0))],
        out_specs=pl.BlockSpec((tm, d), lambda i: (i, 0)),
        out_shape=jax.ShapeDtypeStruct((t, d), F32),
        compiler_params=_params(1),
        name="out_projection",
    )(x2, gate, ga, gb, w_out)


def _rope_tables(pos):
    inv = ROPE_THETA ** (-jnp.arange(0, ROT_DIM, 2, dtype=F32) / ROT_DIM)
    ang = (pos.astype(F32)[:, None] * inv[None, :]).T
    return jnp.cos(ang), jnp.sin(ang)


def kernel(x_prompt, x_sample, cache_a_k, cache_a_v, cache_b_k, cache_b_v, cache_b_logf,
           c_prompt, c_sample, norm_g, w_ada, b_ada, w_in, b_f, qn_a, kn_a,
           lam_q1, lam_k1, lam_q2, lam_k2, subln_g, qn_b, kn_b, w_out):
    depth = norm_g.shape[0]
    bp, seq, d = x_prompt.shape
    bs, tdec, _ = x_sample.shape
    past = cache_a_k.shape[2]
    assert bp == 1 and seq % KV_TILE == 0 and Q_TILE % CHUNK == 0 and seq % OUT_ROWS == 0
    assert past % CHUNK == 0 and tdec == CHUNK and (bs * tdec) % PROJ_ROWS == 0

    xp = x_prompt.reshape(seq, d)
    xs = x_sample.reshape(bs * tdec, d)
    c_rows = bp + bs
    c_all = jnp.concatenate([c_prompt, c_sample, jnp.zeros((16 - c_rows, d), F32)], axis=0)
    rope_p = _rope_tables(jnp.arange(seq))
    rope_s = _rope_tables(jnp.tile(past + jnp.arange(tdec), bs))
    fb_off = 2 * A_QK_WIDTH + 2 * A_WIDTH + 3 * B_WIDTH

    outs_p = [[] for _ in range(5)]
    outs_s = [[] for _ in range(5)]
    for l in range(depth):
        lam_init = 0.8 - 0.6 * math.exp(-0.3 * l)
        w_all = w_in[l].astype(BF16)
        w_zb = w_all[:, fb_off + B_HEADS:]
        wf_t = jnp.concatenate([w_in[l][:, fb_off:fb_off + B_HEADS].T.astype(BF16),
                                jnp.zeros((16 - B_HEADS, d), BF16)], axis=0)
        bf_col = jnp.concatenate([b_f[l], jnp.zeros((16 - B_HEADS,), F32)]).reshape(16, 1)
        gains = tuple(jnp.broadcast_to(jnp.tile(g[l], GROUP_W // g.shape[1])[:, None],
                                       (GROUP_W, PROJ_ROWS))
                      for g in (qn_a, kn_a, qn_b, kn_b))
        lam_vecs = tuple(v[l].reshape(1, A_QK_DIM) for v in (lam_q1, lam_k1, lam_q2, lam_k2))
        w_o = w_out[l].astype(BF16)
        g_row = norm_g[l].reshape(1, d)

        mod = _modulation(c_all, w_ada[l], b_ada[l])
        shift, scale, gate = mod[:, :d], mod[:, d:2 * d], mod[:, 2 * d:]

        (ka_t, va, za, kb_t, vb, zb, lf_t, qa_t, kab, va_t, qb_t, kb_aug, vb_t) = _projection(
            xp, scale[:bp], shift[:bp], g_row, w_all, w_zb, wf_t, bf_col, gains, rope_p, True)
        ga = _attn_a_prompt(qa_t, kab, va_t, za, lam_vecs, subln_g[l].reshape(A_V_DIM, 1), lam_init)
        gb = _attn_b_prompt(qb_t, kb_aug, vb_t, zb)
        xp = _out_projection(xp, gate[:bp], ga, gb, w_o)
        ka = ka_t.reshape(A_HEADS, 2, A_QK_DIM, seq).transpose(3, 0, 1, 2)
        kb = kb_t.reshape(B_HEADS, B_DIM, seq).transpose(2, 0, 1)
        for dst, a in zip(outs_p, (ka.reshape(bp, seq, A_HEADS, 2, A_QK_DIM),
                                   va.reshape(bp, seq, A_HEADS, A_V_DIM),
                                   kb.reshape(bp, seq, B_HEADS, B_DIM),
                                   vb.reshape(bp, seq, B_HEADS, B_DIM),
                                   lf_t.T.reshape(bp, seq, B_HEADS))):
            dst.append(a)

        rep = lambda a: jnp.repeat(a[bp:c_rows], tdec, axis=0)
        (ka, va, za, kb, vb, zb, lf_t, qa, kab, vab, qb, kbb, vbb) = _projection(
            xs, rep(scale), rep(shift), g_row, w_all, w_zb, wf_t, bf_col, gains, rope_s, False)
        lf_new = lf_t.reshape(B_HEADS, bs, tdec).transpose(1, 0, 2)
        lf_cat = jnp.concatenate([cache_b_logf[l].astype(F32).transpose(0, 2, 1), lf_new,
                                  jnp.zeros((bs, B_HEADS, LANES - tdec), F32)], axis=2)
        caches = (cache_a_k[l].transpose(0, 2, 3, 4, 1).reshape(bs, A_QK_WIDTH, past),
                  cache_a_v[l].reshape(bs, past * A_HEADS, A_V_DIM),
                  cache_b_k[l].transpose(0, 2, 3, 1).reshape(bs, B_WIDTH, past),
                  cache_b_v[l].transpose(0, 2, 3, 1).reshape(bs, B_WIDTH, past))
        ga, gb = _attn_decode((qa, kab, vab, za, qb, kbb, vbb, zb), caches, lf_cat,
                              lam_vecs, subln_g[l].reshape(1, A_V_DIM), lam_init, bs, tdec)
        xs = _out_projection(xs, rep(gate), ga, gb, w_o)
        for dst, a in zip(outs_s, (ka.reshape(bs, tdec, A_HEADS, 2, A_QK_DIM),
                                   va.reshape(bs, tdec, A_HEADS, A_V_DIM),
                                   kb.reshape(bs, tdec, B_HEADS, B_DIM),
                                   vb.reshape(bs, tdec, B_HEADS, B_DIM),
                                   lf_t.T.reshape(bs, tdec, B_HEADS))):
            dst.append(a)

    return (xp.reshape(bp, seq, d), xs.reshape(bs, tdec, d),
            *(jnp.stack(o) for o in outs_p), *(jnp.stack(o) for o in outs_s))
```
